```python
import jax, jax.numpy as jnp
from jax import lax
import numpy as np

D_MODEL = 1024
BATCH = 2
SEQ = 8192
DEPTH = 1

HEAD_DIM = 64
MOBA_HEADS = 8
SB_HEADS = 8
MOBA_WIDTH = MOBA_HEADS * HEAD_DIM
SB_WIDTH = SB_HEADS * HEAD_DIM
MOBA_BLOCK = 256
MOBA_TOPK = 3
MOBA_QCHUNK = 64
SB_QBLOCK = 128
ROPE_THETA = 10000.0
D_FF = 2816
CONV_WIDTH = 3
N_BRANCHES = 2
RMS_EPS = 1e-6
NEG = -1e30
IN_SIZES = (MOBA_WIDTH, MOBA_WIDTH, MOBA_WIDTH, SB_WIDTH, SB_WIDTH, SB_WIDTH, N_BRANCHES * D_MODEL)
IN_COLS = sum(IN_SIZES)
IN_SPLITS = tuple(int(s) for s in np.cumsum(IN_SIZES)[:-1])

kernel_name = "hybrid_moba_stickbreaking_convglu"


def rms_norm(x, g):
    xf = x.astype(jnp.float32)
    y = xf * lax.rsqrt(jnp.mean(xf * xf, axis=-1, keepdims=True) + RMS_EPS)
    return (y * g.astype(jnp.float32)).astype(x.dtype)


def to_heads(t, n_heads):
    b, s, _ = t.shape
    return t.reshape(b, s, n_heads, HEAD_DIM).transpose(0, 2, 1, 3)


def from_heads(t):
    b, h, s, d = t.shape
    return t.transpose(0, 2, 1, 3).reshape(b, s, h * d)


def rope(t, pos):
    half = HEAD_DIM // 2
    inv = ROPE_THETA ** (-jnp.arange(half, dtype=jnp.float32) / half)
    ang = pos[:, None] * inv[None, :]
    cos, sin = jnp.cos(ang), jnp.sin(ang)
    tf = t.astype(jnp.float32)
    t1, t2 = tf[..., :half], tf[..., half:]
    out = jnp.concatenate([t1 * cos - t2 * sin, t2 * cos + t1 * sin], axis=-1)
    return out.astype(t.dtype)


def moba_attention(q, k, v):
    b, h, s, d = q.shape
    nb = -(-s // MOBA_BLOCK)
    pad = nb * MOBA_BLOCK - s
    kp = jnp.pad(k, ((0, 0), (0, 0), (0, pad), (0, 0)))
    vp = jnp.pad(v, ((0, 0), (0, 0), (0, pad), (0, 0)))
    k_blk = kp.reshape(b, h, nb, MOBA_BLOCK, d)
    v_blk = vp.reshape(b, h, nb, MOBA_BLOCK, d)
    k_mean = jnp.mean(k_blk.astype(jnp.float32), axis=3)
    gate = jnp.einsum('bhtd,bhnd->bhtn', q.astype(jnp.float32), k_mean)
    q_block = jnp.arange(s) // MOBA_BLOCK
    past = jnp.arange(nb)[None, :] < q_block[:, None]
    gate = jnp.where(past, gate, NEG)
    n_sel = min(MOBA_TOPK, nb)
    _, idx = lax.top_k(gate, n_sel)

    nc = s // MOBA_QCHUNK
    q_c = q.reshape(b, h, nc, MOBA_QCHUNK, d).transpose(2, 0, 1, 3, 4)
    idx_c = idx.reshape(b, h, nc, MOBA_QCHUNK, n_sel).transpose(2, 0, 1, 3, 4)
    scale = HEAD_DIM ** -0.5
    gather = jax.vmap(jax.vmap(lambda blocks, ids: blocks[ids]))

    def chunk(args):
        c, qc, ic = args
        blk = (c * MOBA_QCHUNK) // MOBA_BLOCK
        t = c * MOBA_QCHUNK + jnp.arange(MOBA_QCHUNK)
        k_own = lax.dynamic_index_in_dim(k_blk, blk, axis=2, keepdims=False)
        v_own = lax.dynamic_index_in_dim(v_blk, blk, axis=2, keepdims=False)
        s_own = jnp.einsum('bhqd,bhkd->bhqk', qc, k_own, preferred_element_type=jnp.float32) * scale
        kpos = blk * MOBA_BLOCK + jnp.arange(MOBA_BLOCK)
        s_own = jnp.where(kpos[None, :] <= t[:, None], s_own, NEG)
        k_sel = gather(k_blk, ic)
        v_sel = gather(v_blk, ic)
        s_sel = jnp.einsum('bhqd,bhqnkd->bhqnk', qc, k_sel, preferred_element_type=jnp.float32) * scale
        valid = jnp.arange(n_sel) < blk
        s_sel = jnp.where(valid[:, None], s_sel, NEG)
        logits = jnp.concatenate([s_own, s_sel.reshape(b, h, MOBA_QCHUNK, n_sel * MOBA_BLOCK)], axis=-1)
        p = jax.nn.softmax(logits, axis=-1)
        p_own = p[..., :MOBA_BLOCK].astype(v.dtype)
        p_sel = p[..., MOBA_BLOCK:].reshape(b, h, MOBA_QCHUNK, n_sel, MOBA_BLOCK).astype(v.dtype)
        return (jnp.einsum('bhqk,bhkd->bhqd', p_own, v_own)
                + jnp.einsum('bhqnk,bhqnkd->bhqd', p_sel, v_sel))

    out = lax.map(chunk, (jnp.arange(nc), q_c, idx_c))
    return out.transpose(1, 2, 0, 3, 4).reshape(b, h, s, d)


def stick_breaking_attention(q, k, v):
    b, h, s, d = q.shape
    nc = s // SB_QBLOCK
    q_c = q.reshape(b, h, nc, SB_QBLOCK, d).transpose(2, 0, 1, 3, 4)
    spos = jnp.arange(s)
    scale = HEAD_DIM ** -0.5

    def block(args):
        c, qc = args
        t = c * SB_QBLOCK + jnp.arange(SB_QBLOCK)
        z = jnp.einsum('bhqd,bhkd->bhqk', qc, k, preferred_element_type=jnp.float32) * scale
        causal = spos[None, :] < t[:, None]
        log_1m = jnp.where(causal, -jax.nn.softplus(z), 0.0)
        after = lax.cumsum(log_1m, axis=3, reverse=True) - log_1m
        a = jnp.where(causal, jnp.exp(jax.nn.log_sigmoid(z) + after), 0.0)
        return jnp.einsum('bhqk,bhkd->bhqd', a.astype(v.dtype), v)

    out = lax.map(block, (jnp.arange(nc), q_c))
    return out.transpose(1, 2, 0, 3, 4).reshape(b, h, s, d)


def causal_depthwise_conv(u, w, bias):
    s = u.shape[1]
    up = jnp.pad(u, ((0, 0), (CONV_WIDTH - 1, 0), (0, 0)))
    out = up[:, 0:s] * w[0]
    for i in range(1, CONV_WIDTH):
        out = out + up[:, i:i + s] * w[i]
    return out + bias


def setup_inputs(seed: int = 0) -> dict:
    key = jax.random.key(seed)
    ks = jax.random.split(key, 14)
    f32 = jnp.float32
    nrm = lambda k, shape, fan: jax.random.normal(k, shape, f32) * (fan ** -0.5)
    return {
        "x": jax.random.normal(ks[0], (BATCH, SEQ, D_MODEL), f32),
        "g_mix": 1.0 + 0.01 * jax.random.normal(ks[1], (DEPTH, D_MODEL), f32),
        "w_in": nrm(ks[2], (DEPTH, D_MODEL, IN_COLS), D_MODEL),
        "b_gate": 0.01 * jax.random.normal(ks[3], (DEPTH, N_BRANCHES * D_MODEL), f32),
        "w_branch_a": nrm(ks[4], (DEPTH, MOBA_WIDTH, D_MODEL), MOBA_WIDTH),
        "w_branch_b": nrm(ks[5], (DEPTH, SB_WIDTH, D_MODEL), SB_WIDTH),
        "w_out": nrm(ks[6], (DEPTH, D_MODEL, D_MODEL), D_MODEL),
        "g_ffn": 1.0 + 0.01 * jax.random.normal(ks[7], (DEPTH, D_MODEL), f32),
        "w_up": nrm(ks[8], (DEPTH, D_MODEL, 2 * D_FF), D_MODEL),
        "conv_w": nrm(ks[9], (DEPTH, CONV_WIDTH, 2 * D_FF), CONV_WIDTH),
        "conv_b": 0.01 * jax.random.normal(ks[10], (DEPTH, 2 * D_FF), f32),
        "w_down": nrm(ks[11], (DEPTH, D_FF, D_MODEL), D_FF),
        "g_final": 1.0 + 0.01 * jax.random.normal(ks[12], (D_MODEL,), f32),
    }


def reference(x, g_mix, w_in, b_gate, w_branch_a, w_branch_b, w_out, g_ffn, w_up, conv_w, conv_b, w_down, g_final):
    b, s, _ = x.shape
    pos = jnp.arange(s, dtype=jnp.float32)
    for layer in range(DEPTH):
        h = rms_norm(x, g_mix[layer])
        proj = h @ w_in[layer]
        qa, ka, va, qb, kb, vb, gates = jnp.split(proj, IN_SPLITS, axis=-1)
        qa = rope(to_heads(qa, MOBA_HEADS), pos)
        ka = rope(to_heads(ka, MOBA_HEADS), pos)
        ya = from_heads(moba_attention(qa, ka, to_heads(va, MOBA_HEADS)))
        yb = from_heads(stick_breaking_attention(to_heads(qb, SB_HEADS), to_heads(kb, SB_HEADS),
                                                 to_heads(vb, SB_HEADS)))
        g = jax.nn.sigmoid(gates + b_gate[layer]).reshape(b, s, N_BRANCHES, D_MODEL)
        merged = g[:, :, 0] * (ya @ w_branch_a[layer]) + g[:, :, 1] * (yb @ w_branch_b[layer])
        x = x + merged @ w_out[layer]
        h = rms_norm(x, g_ffn[layer])
        u = causal_depthwise_conv(h @ w_up[layer], conv_w[layer], conv_b[layer])
        u_gate, u_val = jnp.split(u, 2, axis=-1)
        x = x + (jax.nn.silu(u_gate) * u_val) @ w_down[layer]
    return rms_norm(x, g_final)
```

```python
import functools

import numpy as np
import jax
import jax.numpy as jnp
from jax import lax
from jax.experimental import pallas as pl
from jax.experimental.pallas import tpu as pltpu

F32 = jnp.float32
BF16 = jnp.bfloat16

D_MODEL = 1024
HEAD_DIM = 64
HALF = HEAD_DIM // 2
N_HEADS = 8
WIDTH = N_HEADS * HEAD_DIM
BLK = 256
TOPK = 3
ROPE_THETA = 10000.0
D_FF = 2816
RMS_EPS = 1e-6
NEG = -1e30
SCALE = HEAD_DIM ** -0.5

QA_ROWS = 128
VA_ROWS = 80
SB_STOP = 115.0

MERGE_TM = 512
FFN_TM = 512
FFN_TF = D_FF // 2
HALO = 16

VMEM_LIMIT = 48 * 1024 * 1024


def _rms(x, g):
    ms = jnp.mean(x * x, axis=-1, keepdims=True)
    return x * lax.rsqrt(ms + RMS_EPS) * g


def _split_bf16(x):
    hi = x.astype(BF16)
    lo = (x - hi.astype(F32)).astype(BF16)
    return hi, lo


def _dot(a, b):
    return jnp.dot(a, b, preferred_element_type=F32)


def _proj_kernel(nb, x_ref, g_ref, wrow_ref, wt_ref, cq_ref, sq_ref, ck_ref, sk_ref,
                 qa_ref, ka_ref, va_ref, qb_ref, kb_ref, vb_ref, km_ref):
    blk = pl.program_id(0) % nb

    @pl.when(pl.program_id(0) == 0)
    def _():
        km_ref[...] = jnp.zeros_like(km_ref)

    h = _rms(x_ref[...], g_ref[...]).astype(BF16)
    rows = _dot(h, wrow_ref[...])
    cols = lax.dot_general(wt_ref[...], h, (((1,), (1,)), ((), ())),
                           preferred_element_type=F32)

    ka = rows[:, :WIDTH]
    lane = lax.broadcasted_iota(jnp.int32, (BLK, WIDTH), 1)
    first_half = (lane % HEAD_DIM) < HALF
    swapped = jnp.where(first_half, pltpu.roll(ka, WIDTH - HALF, 1), pltpu.roll(ka, HALF, 1))
    ck = jnp.concatenate([ck_ref[...]] * (WIDTH // 128), axis=1)
    sk = jnp.concatenate([sk_ref[...]] * (WIDTH // 128), axis=1)
    ka = ka * ck + swapped * sk
    km_ref[pl.ds(blk, 1), :] = jnp.mean(ka, axis=0, keepdims=True)
    onehot = (lax.broadcasted_iota(jnp.int32, (BLK, HEAD_DIM), 1) == blk).astype(F32)
    ka_blocks = []
    for hd in range(N_HEADS):
        ka_blocks += [ka[:, hd * HEAD_DIM:(hd + 1) * HEAD_DIM], onehot]
    ka_ref[0, 0] = jnp.concatenate(ka_blocks, axis=1).astype(BF16)
    kb_ref[0, 0] = rows[:, WIDTH:].astype(BF16)

    cq = cq_ref[...]
    sq = sq_ref[...]
    km = km_ref[...]
    n_iota = lax.broadcasted_iota(jnp.int32, (nb, BLK), 0).astype(F32)
    blk_f = blk.astype(F32)
    past = n_iota < blk_f
    zeros_q = jnp.zeros((QA_ROWS - HEAD_DIM - nb, BLK), F32)
    zeros_h = jnp.zeros((HEAD_DIM, BLK), F32)
    ones_v = jnp.ones((VA_ROWS - HEAD_DIM, BLK), F32)
    for hd in range(N_HEADS):
        r0 = hd * HEAD_DIM
        t1 = cols[r0:r0 + HALF]
        t2 = cols[r0 + HALF:r0 + HEAD_DIM]
        q = jnp.concatenate([t1 * cq - t2 * sq, t2 * cq + t1 * sq], axis=0)

        q_hi, q_lo = _split_bf16(q)
        km_hi, km_lo = _split_bf16(km[:, r0:r0 + HEAD_DIM])
        gate = _dot(km_hi, q_hi) + (_dot(km_hi, q_lo) + _dot(km_lo, q_hi))
        gate = jnp.where(past, gate, NEG)
        bias = jnp.where(n_iota == blk_f, 0.0, NEG)
        for r in range(TOPK):
            m = jnp.max(gate, axis=0, keepdims=True)
            idx = jnp.min(jnp.where(gate == m, n_iota, float(nb)), axis=0, keepdims=True)
            pick = n_iota == idx
            bias = jnp.where(pick, jnp.where(blk > r, 0.0, bias), bias)
            gate = jnp.where(pick, -jnp.inf, gate)
        qa_ref[0, hd] = jnp.concatenate([q, bias, zeros_q], axis=0).astype(BF16)

        va_ref[0, hd, 0] = jnp.concatenate([cols[WIDTH + r0:WIDTH + r0 + HEAD_DIM], ones_v],
                                           axis=0).astype(BF16)
        qb = cols[2 * WIDTH + r0:2 * WIDTH + r0 + HEAD_DIM] * SCALE
        qb_ref[0, hd] = jnp.concatenate([qb, zeros_h] if hd % 2 == 0 else [zeros_h, qb],
                                        axis=0).astype(BF16)
        vb_ref[0, hd, 0] = cols[3 * WIDTH + r0:3 * WIDTH + r0 + HEAD_DIM].astype(BF16)


def _proj(x2, g_mix, w_row, w_t, cq, sq, ck, sk, batch, seq):
    nb = seq // BLK
    grid = (batch * nb,)
    bi = lambda i: i // nb
    ji = lambda i: i % nb
    out_shape = (
        jax.ShapeDtypeStruct((batch, N_HEADS, QA_ROWS, seq), BF16),
        jax.ShapeDtypeStruct((batch, nb, BLK, N_HEADS * 128), BF16),
        jax.ShapeDtypeStruct((batch, N_HEADS, nb, VA_ROWS, BLK), BF16),
        jax.ShapeDtypeStruct((batch, N_HEADS, 2 * HEAD_DIM, seq), BF16),
        jax.ShapeDtypeStruct((batch, nb, BLK, WIDTH), BF16),
        jax.ShapeDtypeStruct((batch, N_HEADS, nb, HEAD_DIM, BLK), BF16),
    )
    in_specs = [
        pl.BlockSpec((BLK, D_MODEL), lambda i: (i, 0)),
        pl.BlockSpec((1, D_MODEL), lambda i: (0, 0)),
        pl.BlockSpec((D_MODEL, 2 * WIDTH), lambda i: (0, 0)),
        pl.BlockSpec((4 * WIDTH, D_MODEL), lambda i: (0, 0)),
        pl.BlockSpec((HALF, BLK), lambda i: (0, ji(i))),
        pl.BlockSpec((HALF, BLK), lambda i: (0, ji(i))),
        pl.BlockSpec((BLK, 128), lambda i: (ji(i), 0)),
        pl.BlockSpec((BLK, 128), lambda i: (ji(i), 0)),
    ]
    out_specs = (
        pl.BlockSpec((1, N_HEADS, QA_ROWS, BLK), lambda i: (bi(i), 0, 0, ji(i))),
        pl.BlockSpec((1, 1, BLK, N_HEADS * 128), lambda i: (bi(i), ji(i), 0, 0)),
        pl.BlockSpec((1, N_HEADS, 1, VA_ROWS, BLK), lambda i: (bi(i), 0, ji(i), 0, 0)),
        pl.BlockSpec((1, N_HEADS, 2 * HEAD_DIM, BLK), lambda i: (bi(i), 0, 0, ji(i))),
        pl.BlockSpec((1, 1, BLK, WIDTH), lambda i: (bi(i), ji(i), 0, 0)),
        pl.BlockSpec((1, N_HEADS, 1, HEAD_DIM, BLK), lambda i: (bi(i), 0, ji(i), 0, 0)),
    )
    return pl.pallas_call(
        functools.partial(_proj_kernel, nb),
        grid=grid, in_specs=in_specs, out_specs=out_specs, out_shape=out_shape,
        scratch_shapes=[pltpu.VMEM((nb, WIDTH), F32)],
        compiler_params=pltpu.CompilerParams(dimension_semantics=("arbitrary",),
                                             vmem_limit_bytes=VMEM_LIMIT),
        name="proj",
    )(x2, g_mix, w_row, w_t, cq, sq, ck, sk)


def _moba_kernel(q_ref, k_ref, v_ref, o_ref):
    i = pl.program_id(2)
    k_iota = lax.broadcasted_iota(jnp.int32, (BLK, BLK), 0)
    q_iota = lax.broadcasted_iota(jnp.int32, (BLK, BLK), 1)
    causal = k_iota <= q_iota

    state = []
    for hh in range(2):
        s = _dot(k_ref[0, i, :, hh * 128:(hh + 1) * 128], q_ref[0, hh])
        s = jnp.where(causal, s, NEG)
        m = jnp.max(s, axis=0, keepdims=True)
        p = jnp.exp(s - m).astype(BF16)
        state += [m, _dot(v_ref[0, hh, i], p)]

    def body(j, carry):
        out = []
        for hh in range(2):
            m, acc = carry[2 * hh], carry[2 * hh + 1]
            s = _dot(k_ref[0, j, :, hh * 128:(hh + 1) * 128], q_ref[0, hh])
            m_new = jnp.maximum(m, jnp.max(s, axis=0, keepdims=True))
            p = jnp.exp(s - m_new).astype(BF16)
            acc = acc * jnp.exp(m - m_new) + _dot(v_ref[0, hh, j], p)
            out += [m_new, acc]
        return tuple(out)

    res = lax.fori_loop(0, i, body, tuple(state))
    outs = []
    for hh in range(2):
        acc = res[2 * hh + 1]
        outs.append((acc[:HEAD_DIM] / acc[HEAD_DIM:HEAD_DIM + 1]).T)
    o_ref[0] = jnp.concatenate(outs, axis=1).astype(BF16)


def _moba(qa, ka, va, batch, seq):
    nb = seq // BLK
    return pl.pallas_call(
        _moba_kernel,
        grid=(batch, N_HEADS // 2, nb),
        in_specs=[
            pl.BlockSpec((1, 2, QA_ROWS, BLK), lambda b, p, i: (b, p, 0, i)),
            pl.BlockSpec((1, nb, BLK, 256), lambda b, p, i: (b, 0, 0, p)),
            pl.BlockSpec((1, 2, nb, VA_ROWS, BLK), lambda b, p, i: (b, p, 0, 0, 0)),
        ],
        out_specs=pl.BlockSpec((1, BLK, 128), lambda b, p, i: (b, i, p)),
        out_shape=jax.ShapeDtypeStruct((batch, seq, WIDTH), BF16),
        compiler_params=pltpu.CompilerParams(
            dimension_semantics=("arbitrary", "arbitrary", "arbitrary"),
            vmem_limit_bytes=VMEM_LIMIT),
        name="moba",
    )(qa, ka, va)


def _sb_kernel(q_ref, k_ref, v_ref, o_ref):
    i = pl.program_id(2)
    k_iota = lax.broadcasted_iota(jnp.int32, (BLK, BLK), 0)
    q_iota = lax.broadcasted_iota(jnp.int32, (BLK, BLK), 1)
    strict = k_iota < q_iota
    later = (q_iota > k_iota).astype(BF16)

    outs = []
    for hh in range(2):
        q_t = q_ref[0, hh]

        def step(j, run, acc, diag):
            z = _dot(k_ref[0, j], q_t)
            sp = jnp.maximum(z, 0.0) + jnp.log1p(jnp.exp(-jnp.abs(z)))
            lg = -sp
            if diag:
                lg = jnp.where(strict, lg, 0.0)
            lg_hi, lg_lo = _split_bf16(lg)
            after = _dot(later, lg_hi) + _dot(later, lg_lo)
            a = jnp.exp((z - sp) + (after + run))
            if diag:
                a = jnp.where(strict, a, 0.0)
            acc = acc + _dot(v_ref[0, hh, j], a.astype(BF16))
            run = run + (after[0:1] + lg[0:1])
            return run, acc

        run, acc = step(i, jnp.zeros((1, BLK), F32), jnp.zeros((HEAD_DIM, BLK), F32), True)

        def cond(c):
            return jnp.logical_and(c[0] >= 0, jnp.max(c[1]) > -SB_STOP)

        def body(c):
            run, acc = step(c[0], c[1], c[2], False)
            return c[0] - 1, run, acc

        acc = lax.while_loop(cond, body, (i - 1, run, acc))[2]
        outs.append(acc.T)
    o_ref[0] = jnp.concatenate(outs, axis=1).astype(BF16)


def _sb(qb, kb, vb, batch, seq):
    nb = seq // BLK
    return pl.pallas_call(
        _sb_kernel,
        grid=(batch, N_HEADS // 2, nb),
        in_specs=[
            pl.BlockSpec((1, 2, 2 * HEAD_DIM, BLK), lambda b, p, i: (b, p, 0, i)),
            pl.BlockSpec((1, nb, BLK, 128), lambda b, p, i: (b, 0, 0, p)),
            pl.BlockSpec((1, 2, nb, HEAD_DIM, BLK), lambda b, p, i: (b, p, 0, 0, 0)),
        ],
        out_specs=pl.BlockSpec((1, BLK, 128), lambda b, p, i: (b, i, p)),
        out_shape=jax.ShapeDtypeStruct((batch, seq, WIDTH), BF16),
        compiler_params=pltpu.CompilerParams(
            dimension_semantics=("arbitrary", "arbitrary", "arbitrary"),
            vmem_limit_bytes=VMEM_LIMIT),
        name="sb",
    )(qb, kb, vb)


def _merge_kernel(x_ref, ya_ref, yb_ref, gm_ref, wg_ref, bg_ref, wa_ref, wb_ref, wo_ref, gf_ref,
                  x1_ref, h2_ref):
    x = x_ref[...]
    h = _rms(x, gm_ref[...]).astype(BF16)
    gates = jax.nn.sigmoid(_dot(h, wg_ref[...]) + bg_ref[...])
    merged = (gates[:, :D_MODEL] * _dot(ya_ref[...], wa_ref[...])
              + gates[:, D_MODEL:] * _dot(yb_ref[...], wb_ref[...]))
    x1 = x + _dot(merged.astype(BF16), wo_ref[...])
    x1_ref[...] = x1
    h2_ref[...] = _rms(x1, gf_ref[...]).astype(BF16)


def _merge(x2, ya, yb, g_mix, w_gate, b_gate, w_a, w_b, w_out, g_ffn):
    t = x2.shape[0]
    tm = MERGE_TM
    full = lambda shape: pl.BlockSpec(shape, lambda i: (0, 0))
    return pl.pallas_call(
        _merge_kernel,
        grid=(t // tm,),
        in_specs=[
            pl.BlockSpec((tm, D_MODEL), lambda i: (i, 0)),
            pl.BlockSpec((tm, WIDTH), lambda i: (i, 0)),
            pl.BlockSpec((tm, WIDTH), lambda i: (i, 0)),
            full((1, D_MODEL)),
            full((D_MODEL, 2 * D_MODEL)),
            full((1, 2 * D_MODEL)),
            full((WIDTH, D_MODEL)),
            full((WIDTH, D_MODEL)),
            full((D_MODEL, D_MODEL)),
            full((1, D_MODEL)),
        ],
        out_specs=(pl.BlockSpec((tm, D_MODEL), lambda i: (i, 0)),
                   pl.BlockSpec((tm, D_MODEL), lambda i: (i, 0))),
        out_shape=(jax.ShapeDtypeStruct((t, D_MODEL), F32),
                   jax.ShapeDtypeStruct((t, D_MODEL), BF16)),
        compiler_params=pltpu.CompilerParams(dimension_semantics=("arbitrary",),
                                             vmem_limit_bytes=VMEM_LIMIT),
        name="merge",
    )(x2, ya, yb, g_mix, w_gate, b_gate, w_a, w_b, w_out, g_ffn)


def _ffn_kernel(tiles_per_seq, h_ref, halo_ref, x1_ref, wug_ref, wuv_ref, cwg_ref, cwv_ref,
                cbg_ref, cbv_ref, wd_ref, gfin_ref, o_ref, acc_ref):
    i = pl.program_id(0)
    c = pl.program_id(1)
    tm = h_ref.shape[0]
    halo = jnp.where(i % tiles_per_seq == 0, jnp.zeros_like(halo_ref[...]), halo_ref[...])
    hx = jnp.concatenate([halo, h_ref[...]], axis=0)

    def conv(w_ref, cw_ref, cb_ref):
        u = _dot(hx, w_ref[...])
        cw = cw_ref[...]
        out = (pltpu.roll(u, 2, 0) * cw[0:1] + pltpu.roll(u, 1, 0) * cw[1:2] + u * cw[2:3])
        return out[HALO:] + cb_ref[...]

    ug = conv(wug_ref, cwg_ref, cbg_ref)
    uv = conv(wuv_ref, cwv_ref, cbv_ref)
    act = (ug * jax.nn.sigmoid(ug) * uv).astype(BF16)
    part = _dot(act, wd_ref[...])

    @pl.when(c == 0)
    def _():
        acc_ref[...] = x1_ref[...] + part

    @pl.when(c > 0)
    def _():
        acc_ref[...] += part

    @pl.when(c == pl.num_programs(1) - 1)
    def _():
        o_ref[...] = _rms(acc_ref[...], gfin_ref[...])


def _ffn(h2, x1, w_up, conv_w, conv_b, w_down, g_final, seq):
    t = h2.shape[0]
    tm, tf = FFN_TM, FFN_TF
    nc = D_FF // tf
    hb = tm // HALO
    return pl.pallas_call(
        functools.partial(_ffn_kernel, seq // tm),
        grid=(t // tm, nc),
        in_specs=[
            pl.BlockSpec((tm, D_MODEL), lambda i, c: (i, 0)),
            pl.BlockSpec((HALO, D_MODEL), lambda i, c: (jnp.maximum(i * hb - 1, 0), 0)),
            pl.BlockSpec((tm, D_MODEL), lambda i, c: (i, 0)),
            pl.BlockSpec((D_MODEL, tf), lambda i, c: (0, c)),
            pl.BlockSpec((D_MODEL, tf), lambda i, c: (0, nc + c)),
            pl.BlockSpec((TOPK, tf), lambda i, c: (0, c)),
            pl.BlockSpec((TOPK, tf), lambda i, c: (0, nc + c)),
            pl.BlockSpec((1, tf), lambda i, c: (0, c)),
            pl.BlockSpec((1, tf), lambda i, c: (0, nc + c)),
            pl.BlockSpec((tf, D_MODEL), lambda i, c: (c, 0)),
            pl.BlockSpec((1, D_MODEL), lambda i, c: (0, 0)),
        ],
        out_specs=pl.BlockSpec((tm, D_MODEL), lambda i, c: (i, 0)),
        out_shape=jax.ShapeDtypeStruct((t, D_MODEL), F32),
        scratch_shapes=[pltpu.VMEM((tm, D_MODEL), F32)],
        compiler_params=pltpu.CompilerParams(dimension_semantics=("arbitrary", "arbitrary"),
                                             vmem_limit_bytes=VMEM_LIMIT),
        name="ffn",
    )(h2, h2, x1, w_up, w_up, conv_w, conv_w, conv_b, conv_b, w_down, g_final)


def _rope_tables(seq):
    inv = ROPE_THETA ** (-jnp.arange(HALF, dtype=F32) / HALF)
    ang = jnp.arange(seq, dtype=F32)[:, None] * inv[None, :]
    cos, sin = jnp.cos(ang), jnp.sin(ang)
    ck = jnp.concatenate([cos, cos, cos, cos], axis=1)
    sk = jnp.concatenate([-sin, sin, -sin, sin], axis=1)
    return (cos.T * SCALE), (sin.T * SCALE), ck, sk


def kernel(x, g_mix, w_in, b_gate, w_branch_a, w_branch_b, w_out, g_ffn, w_up, conv_w, conv_b,
           w_down, g_final):
    batch, seq, _ = x.shape
    assert seq % BLK == 0 and seq % FFN_TM == 0 and seq // BLK <= HALF
    assert g_mix.shape[0] == 1
    x2 = x.reshape(batch * seq, D_MODEL)
    cq, sq, ck, sk = _rope_tables(seq)
    for layer in range(1):
        w = w_in[layer]
        col = lambda k: w[:, k * WIDTH:(k + 1) * WIDTH]
        w_row = jnp.concatenate([col(1), col(4)], axis=1).astype(BF16)
        w_t = jnp.concatenate([col(0), col(2), col(3), col(5)], axis=1).T.astype(BF16)
        w_gate = w[:, 6 * WIDTH:].astype(BF16)
        gm = g_mix[layer][None, :]
        qa, ka, va, qb, kb, vb = _proj(x2, gm, w_row, w_t, cq, sq, ck, sk, batch, seq)
        ya = _moba(qa, ka, va, batch, seq).reshape(batch * seq, WIDTH)
        yb = _sb(qb, kb, vb, batch, seq).reshape(batch * seq, WIDTH)
        x1, h2 = _merge(x2, ya, yb, gm, w_gate, b_gate[layer][None, :],
                        w_branch_a[layer].astype(BF16), w_branch_b[layer].astype(BF16),
                        w_out[layer].astype(BF16), g_ffn[layer][None, :])
        x2 = _ffn(h2, x1, w_up[layer].astype(BF16), conv_w[layer], conv_b[layer][None, :],
                  w_down[layer].astype(BF16), g_final[None, :], seq)
    return x2.reshape(batch, seq, D_MODEL)
```

```python
import functools

import numpy as np
import jax
import jax.numpy as jnp
from jax import lax
from jax.experimental import pallas as pl
from jax.experimental.pallas import tpu as pltpu

F32 = jnp.float32
BF16 = jnp.bfloat16

D_MODEL = 1024
HEAD_DIM = 64
HALF = HEAD_DIM // 2
N_HEADS = 8
WIDTH = N_HEADS * HEAD_DIM
BLK = 256
TOPK = 3
ROPE_THETA = 10000.0
D_FF = 2816
RMS_EPS = 1e-6
NEG = -1e30
SCALE = HEAD_DIM ** -0.5

QA_ROWS = 128
VA_ROWS = 80
MOBA_CH = 2
SB_STOP = 115.0

MERGE_TM = 512
FFN_TM = 512
FFN_TF = D_FF // 2
HALO = 16

VMEM_LIMIT = 48 * 1024 * 1024


def _rms(x, g):
    ms = jnp.mean(x * x, axis=-1, keepdims=True)
    return x * lax.rsqrt(ms + RMS_EPS) * g


def _split_bf16(x):
    hi = x.astype(BF16)
    lo = (x - hi.astype(F32)).astype(BF16)
    return hi, lo


def _dot(a, b):
    return jnp.dot(a, b, preferred_element_type=F32)


def _proj_kernel(nb, x_ref, g_ref, wrow_ref, wt_ref, cq_ref, sq_ref, ck_ref, sk_ref,
                 qa_ref, ka_ref, va_ref, qb_ref, kb_ref, vb_ref, km_ref):
    blk = pl.program_id(0) % nb

    @pl.when(pl.program_id(0) == 0)
    def _():
        km_ref[...] = jnp.zeros_like(km_ref)

    h = _rms(x_ref[...], g_ref[...]).astype(BF16)
    rows = _dot(h, wrow_ref[...])
    cols = lax.dot_general(wt_ref[...], h, (((1,), (1,)), ((), ())),
                           preferred_element_type=F32)

    ka = rows[:, :WIDTH]
    lane = lax.broadcasted_iota(jnp.int32, (BLK, WIDTH), 1)
    first_half = (lane % HEAD_DIM) < HALF
    swapped = jnp.where(first_half, pltpu.roll(ka, WIDTH - HALF, 1), pltpu.roll(ka, HALF, 1))
    ck = jnp.concatenate([ck_ref[...]] * (WIDTH // 128), axis=1)
    sk = jnp.concatenate([sk_ref[...]] * (WIDTH // 128), axis=1)
    ka = ka * ck + swapped * sk
    km_ref[pl.ds(blk, 1), :] = jnp.mean(ka, axis=0, keepdims=True)
    onehot = (lax.broadcasted_iota(jnp.int32, (BLK, HEAD_DIM), 1) == blk).astype(F32)
    ka_blocks = []
    for hd in range(N_HEADS):
        ka_blocks += [ka[:, hd * HEAD_DIM:(hd + 1) * HEAD_DIM], onehot]
    ka_ref[0, 0] = jnp.concatenate(ka_blocks, axis=1).astype(BF16)
    kb_ref[0, 0] = rows[:, WIDTH:].astype(BF16)

    cq = cq_ref[...]
    sq = sq_ref[...]
    km = km_ref[...]
    n_iota = lax.broadcasted_iota(jnp.int32, (nb, BLK), 0).astype(F32)
    blk_f = blk.astype(F32)
    past = n_iota < blk_f
    zeros_q = jnp.zeros((QA_ROWS - HEAD_DIM - nb, BLK), F32)
    zeros_h = jnp.zeros((HEAD_DIM, BLK), F32)
    ones_v = jnp.ones((VA_ROWS - HEAD_DIM, BLK), F32)
    for hd in range(N_HEADS):
        r0 = hd * HEAD_DIM
        t1 = cols[r0:r0 + HALF]
        t2 = cols[r0 + HALF:r0 + HEAD_DIM]
        q = jnp.concatenate([t1 * cq - t2 * sq, t2 * cq + t1 * sq], axis=0)

        q_hi, q_lo = _split_bf16(q)
        km_hi, km_lo = _split_bf16(km[:, r0:r0 + HEAD_DIM])
        gate = _dot(km_hi, q_hi) + (_dot(km_hi, q_lo) + _dot(km_lo, q_hi))
        gate = jnp.where(past, gate, NEG)
        bias = jnp.where(n_iota == blk_f, 0.0, NEG)
        for r in range(TOPK):
            m = jnp.max(gate, axis=0, keepdims=True)
            idx = jnp.min(jnp.where(gate == m, n_iota, float(nb)), axis=0, keepdims=True)
            pick = n_iota == idx
            bias = jnp.where(pick, jnp.where(blk > r, 0.0, bias), bias)
            gate = jnp.where(pick, -jnp.inf, gate)
        qa_ref[0, hd] = jnp.concatenate([q, bias, zeros_q], axis=0).astype(BF16)

        va_ref[0, hd, 0] = jnp.concatenate([cols[WIDTH + r0:WIDTH + r0 + HEAD_DIM], ones_v],
                                           axis=0).astype(BF16)
        qb = cols[2 * WIDTH + r0:2 * WIDTH + r0 + HEAD_DIM] * SCALE
        qb_ref[0, hd] = jnp.concatenate([qb, zeros_h] if hd % 2 == 0 else [zeros_h, qb],
                                        axis=0).astype(BF16)
        vb_ref[0, hd, 0] = cols[3 * WIDTH + r0:3 * WIDTH + r0 + HEAD_DIM].astype(BF16)


def _proj(x2, g_mix, w_row, w_t, cq, sq, ck, sk, batch, seq):
    nb = seq // BLK
    grid = (batch * nb,)
    bi = lambda i: i // nb
    ji = lambda i: i % nb
    out_shape = (
        jax.ShapeDtypeStruct((batch, N_HEADS, QA_ROWS, seq), BF16),
        jax.ShapeDtypeStruct((batch, nb, BLK, N_HEADS * 128), BF16),
        jax.ShapeDtypeStruct((batch, N_HEADS, nb // MOBA_CH, VA_ROWS, MOBA_CH * BLK), BF16),
        jax.ShapeDtypeStruct((batch, N_HEADS, 2 * HEAD_DIM, seq), BF16),
        jax.ShapeDtypeStruct((batch, nb, BLK, WIDTH), BF16),
        jax.ShapeDtypeStruct((batch, N_HEADS, nb, HEAD_DIM, BLK), BF16),
    )
    in_specs = [
        pl.BlockSpec((BLK, D_MODEL), lambda i: (i, 0)),
        pl.BlockSpec((1, D_MODEL), lambda i: (0, 0)),
        pl.BlockSpec((D_MODEL, 2 * WIDTH), lambda i: (0, 0)),
        pl.BlockSpec((4 * WIDTH, D_MODEL), lambda i: (0, 0)),
        pl.BlockSpec((HALF, BLK), lambda i: (0, ji(i))),
        pl.BlockSpec((HALF, BLK), lambda i: (0, ji(i))),
        pl.BlockSpec((BLK, 128), lambda i: (ji(i), 0)),
        pl.BlockSpec((BLK, 128), lambda i: (ji(i), 0)),
    ]
    out_specs = (
        pl.BlockSpec((1, N_HEADS, QA_ROWS, BLK), lambda i: (bi(i), 0, 0, ji(i))),
        pl.BlockSpec((1, 1, BLK, N_HEADS * 128), lambda i: (bi(i), ji(i), 0, 0)),
        pl.BlockSpec((1, N_HEADS, 1, VA_ROWS, BLK),
                     lambda i: (bi(i), 0, ji(i) // MOBA_CH, 0, ji(i) % MOBA_CH)),
        pl.BlockSpec((1, N_HEADS, 2 * HEAD_DIM, BLK), lambda i: (bi(i), 0, 0, ji(i))),
        pl.BlockSpec((1, 1, BLK, WIDTH), lambda i: (bi(i), ji(i), 0, 0)),
        pl.BlockSpec((1, N_HEADS, 1, HEAD_DIM, BLK), lambda i: (bi(i), 0, ji(i), 0, 0)),
    )
    return pl.pallas_call(
        functools.partial(_proj_kernel, nb),
        grid=grid, in_specs=in_specs, out_specs=out_specs, out_shape=out_shape,
        scratch_shapes=[pltpu.VMEM((nb, WIDTH), F32)],
        compiler_params=pltpu.CompilerParams(dimension_semantics=("arbitrary",),
                                             vmem_limit_bytes=VMEM_LIMIT),
        name="proj",
    )(x2, g_mix, w_row, w_t, cq, sq, ck, sk)


def _moba_kernel(q_ref, k_ref, v_ref, o_ref, s_ref, m_ref, acc_ref):
    i = pl.program_id(2)
    n_past = (i + MOBA_CH) // MOBA_CH - 1
    rows = MOBA_CH * BLK
    q_pos = i * BLK + lax.broadcasted_iota(jnp.int32, (rows, BLK), 1)
    k_off = lax.broadcasted_iota(jnp.int32, (rows, BLK), 0)

    def scores(c, slot):
        for hh in range(2):
            for u in range(MOBA_CH):
                s_ref[slot, hh, u * BLK:(u + 1) * BLK, :] = _dot(
                    k_ref[0, c * MOBA_CH + u, :, hh * 128:(hh + 1) * 128], q_ref[0, hh])

    def attend(c, slot, causal):
        for hh in range(2):
            s = s_ref[slot, hh]
            if causal:
                s = jnp.where(c * rows + k_off <= q_pos, s, NEG)
            m = m_ref[hh]
            m_new = jnp.maximum(m, jnp.max(s, axis=0, keepdims=True))
            p = jnp.exp(s - m_new).astype(BF16)
            acc_ref[hh] = acc_ref[hh] * jnp.exp(m - m_new) + _dot(v_ref[0, hh, c], p)
            m_ref[hh] = m_new

    m_ref[...] = jnp.full_like(m_ref, NEG)
    acc_ref[...] = jnp.zeros_like(acc_ref)
    scores(n_past, 0)
    scores(0, 1)
    attend(n_past, 0, True)

    def body(j, carry):
        scores(2 * j + 1, 0)
        attend(2 * j, 1, False)
        scores(jnp.minimum(2 * j + 2, n_past - 1), 1)
        attend(2 * j + 1, 0, False)
        return carry

    lax.fori_loop(0, n_past // 2, body, 0)

    @pl.when(n_past % 2 == 1)
    def _():
        attend(n_past - 1, 1, False)

    outs = []
    for hh in range(2):
        acc = acc_ref[hh]
        outs.append((acc[:HEAD_DIM] / acc[HEAD_DIM:HEAD_DIM + 1]).T)
    o_ref[0] = jnp.concatenate(outs, axis=1).astype(BF16)


def _moba(qa, ka, va, batch, seq):
    nb = seq // BLK
    nc = nb // MOBA_CH
    return pl.pallas_call(
        _moba_kernel,
        grid=(batch, N_HEADS // 2, nb),
        in_specs=[
            pl.BlockSpec((1, 2, QA_ROWS, BLK), lambda b, p, i: (b, p, 0, i)),
            pl.BlockSpec((1, nb, BLK, 256), lambda b, p, i: (b, 0, 0, p)),
            pl.BlockSpec((1, 2, nc, VA_ROWS, MOBA_CH * BLK), lambda b, p, i: (b, p, 0, 0, 0)),
        ],
        out_specs=pl.BlockSpec((1, BLK, 128), lambda b, p, i: (b, i, p)),
        out_shape=jax.ShapeDtypeStruct((batch, seq, WIDTH), BF16),
        scratch_shapes=[pltpu.VMEM((2, 2, MOBA_CH * BLK, BLK), F32),
                        pltpu.VMEM((2, 1, BLK), F32),
                        pltpu.VMEM((2, VA_ROWS, BLK), F32)],
        compiler_params=pltpu.CompilerParams(
            dimension_semantics=("arbitrary", "arbitrary", "arbitrary"),
            vmem_limit_bytes=VMEM_LIMIT),
        name="moba",
    )(qa, ka, va)


def _sb_kernel(q_ref, k_ref, v_ref, o_ref):
    i = pl.program_id(2)
    k_iota = lax.broadcasted_iota(jnp.int32, (BLK, BLK), 0)
    q_iota = lax.broadcasted_iota(jnp.int32, (BLK, BLK), 1)
    strict = k_iota < q_iota
    later = (q_iota > k_iota).astype(BF16)

    outs = []
    for hh in range(2):
        q_t = q_ref[0, hh]

        def step(j, run, acc, diag):
            z = _dot(k_ref[0, j], q_t)
            sp = jnp.maximum(z, 0.0) + jnp.log1p(jnp.exp(-jnp.abs(z)))
            lg = -sp
            if diag:
                lg = jnp.where(strict, lg, 0.0)
            lg_hi, lg_lo = _split_bf16(lg)
            after = _dot(later, lg_hi) + _dot(later, lg_lo)
            a = jnp.exp((z - sp) + (after + run))
            if diag:
                a = jnp.where(strict, a, 0.0)
            acc = acc + _dot(v_ref[0, hh, j], a.astype(BF16))
            run = run + (after[0:1] + lg[0:1])
            return run, acc

        run, acc = step(i, jnp.zeros((1, BLK), F32), jnp.zeros((HEAD_DIM, BLK), F32), True)

        def cond(c):
            return jnp.logical_and(c[0] >= 0, jnp.max(c[1]) > -SB_STOP)

        def body(c):
            run, acc = step(c[0], c[1], c[2], False)
            return c[0] - 1, run, acc

        acc = lax.while_loop(cond, body, (i - 1, run, acc))[2]
        outs.append(acc.T)
    o_ref[0] = jnp.concatenate(outs, axis=1).astype(BF16)


def _sb(qb, kb, vb, batch, seq):
    nb = seq // BLK
    return pl.pallas_call(
        _sb_kernel,
        grid=(batch, N_HEADS // 2, nb),
        in_specs=[
            pl.BlockSpec((1, 2, 2 * HEAD_DIM, BLK), lambda b, p, i: (b, p, 0, i)),
            pl.BlockSpec((1, nb, BLK, 128), lambda b, p, i: (b, 0, 0, p)),
            pl.BlockSpec((1, 2, nb, HEAD_DIM, BLK), lambda b, p, i: (b, p, 0, 0, 0)),
        ],
        out_specs=pl.BlockSpec((1, BLK, 128), lambda b, p, i: (b, i, p)),
        out_shape=jax.ShapeDtypeStruct((batch, seq, WIDTH), BF16),
        compiler_params=pltpu.CompilerParams(
            dimension_semantics=("arbitrary", "arbitrary", "arbitrary"),
            vmem_limit_bytes=VMEM_LIMIT),
        name="sb",
    )(qb, kb, vb)


def _merge_kernel(x_ref, ya_ref, yb_ref, gm_ref, wg_ref, bg_ref, wa_ref, wb_ref, wo_ref, gf_ref,
                  x1_ref, h2_ref):
    x = x_ref[...]
    h = _rms(x, gm_ref[...]).astype(BF16)
    gates = jax.nn.sigmoid(_dot(h, wg_ref[...]) + bg_ref[...])
    merged = (gates[:, :D_MODEL] * _dot(ya_ref[...], wa_ref[...])
              + gates[:, D_MODEL:] * _dot(yb_ref[...], wb_ref[...]))
    x1 = x + _dot(merged.astype(BF16), wo_ref[...])
    x1_ref[...] = x1
    h2_ref[...] = _rms(x1, gf_ref[...]).astype(BF16)


def _merge(x2, ya, yb, g_mix, w_gate, b_gate, w_a, w_b, w_out, g_ffn):
    t = x2.shape[0]
    tm = MERGE_TM
    full = lambda shape: pl.BlockSpec(shape, lambda i: (0, 0))
    return pl.pallas_call(
        _merge_kernel,
        grid=(t // tm,),
        in_specs=[
            pl.BlockSpec((tm, D_MODEL), lambda i: (i, 0)),
            pl.BlockSpec((tm, WIDTH), lambda i: (i, 0)),
            pl.BlockSpec((tm, WIDTH), lambda i: (i, 0)),
            full((1, D_MODEL)),
            full((D_MODEL, 2 * D_MODEL)),
            full((1, 2 * D_MODEL)),
            full((WIDTH, D_MODEL)),
            full((WIDTH, D_MODEL)),
            full((D_MODEL, D_MODEL)),
            full((1, D_MODEL)),
        ],
        out_specs=(pl.BlockSpec((tm, D_MODEL), lambda i: (i, 0)),
                   pl.BlockSpec((tm, D_MODEL), lambda i: (i, 0))),
        out_shape=(jax.ShapeDtypeStruct((t, D_MODEL), F32),
                   jax.ShapeDtypeStruct((t, D_MODEL), BF16)),
        compiler_params=pltpu.CompilerParams(dimension_semantics=("arbitrary",),
                                             vmem_limit_bytes=VMEM_LIMIT),
        name="merge",
    )(x2, ya, yb, g_mix, w_gate, b_gate, w_a, w_b, w_out, g_ffn)


def _ffn_kernel(tiles_per_seq, h_ref, halo_ref, x1_ref, wug_ref, wuv_ref, cwg_ref, cwv_ref,
                cbg_ref, cbv_ref, wd_ref, gfin_ref, o_ref, acc_ref):
    i = pl.program_id(0)
    c = pl.program_id(1)
    tm = h_ref.shape[0]
    halo = jnp.where(i % tiles_per_seq == 0, jnp.zeros_like(halo_ref[...]), halo_ref[...])
    hx = jnp.concatenate([halo, h_ref[...]], axis=0)

    def conv(w_ref, cw_ref, cb_ref):
        u = _dot(hx, w_ref[...])
        cw = cw_ref[...]
        out = (pltpu.roll(u, 2, 0) * cw[0:1] + pltpu.roll(u, 1, 0) * cw[1:2] + u * cw[2:3])
        return out[HALO:] + cb_ref[...]

    ug = conv(wug_ref, cwg_ref, cbg_ref)
    uv = conv(wuv_ref, cwv_ref, cbv_ref)
    act = (ug * jax.nn.sigmoid(ug) * uv).astype(BF16)
    part = _dot(act, wd_ref[...])

    @pl.when(c == 0)
    def _():
        acc_ref[...] = x1_ref[...] + part

    @pl.when(c > 0)
    def _():
        acc_ref[...] += part

    @pl.when(c == pl.num_programs(1) - 1)
    def _():
        o_ref[...] = _rms(acc_ref[...], gfin_ref[...])


def _ffn(h2, x1, w_up, conv_w, conv_b, w_down, g_final, seq):
    t = h2.shape[0]
    tm, tf = FFN_TM, FFN_TF
    nc = D_FF // tf
    hb = tm // HALO
    return pl.pallas_call(
        functools.partial(_ffn_kernel, seq // tm),
        grid=(t // tm, nc),
        in_specs=[
            pl.BlockSpec((tm, D_MODEL), lambda i, c: (i, 0)),
            pl.BlockSpec((HALO, D_MODEL), lambda i, c: (jnp.maximum(i * hb - 1, 0), 0)),
            pl.BlockSpec((tm, D_MODEL), lambda i, c: (i, 0)),
            pl.BlockSpec((D_MODEL, tf), lambda i, c: (0, c)),
            pl.BlockSpec((D_MODEL, tf), lambda i, c: (0, nc + c)),
            pl.BlockSpec((TOPK, tf), lambda i, c: (0, c)),
            pl.BlockSpec((TOPK, tf), lambda i, c: (0, nc + c)),
            pl.BlockSpec((1, tf), lambda i, c: (0, c)),
            pl.BlockSpec((1, tf), lambda i, c: (0, nc + c)),
            pl.BlockSpec((tf, D_MODEL), lambda i, c: (c, 0)),
            pl.BlockSpec((1, D_MODEL), lambda i, c: (0, 0)),
        ],
        out_specs=pl.BlockSpec((tm, D_MODEL), lambda i, c: (i, 0)),
        out_shape=jax.ShapeDtypeStruct((t, D_MODEL), F32),
        scratch_shapes=[pltpu.VMEM((tm, D_MODEL), F32)],
        compiler_params=pltpu.CompilerParams(dimension_semantics=("arbitrary", "arbitrary"),
                                             vmem_limit_bytes=VMEM_LIMIT),
        name="ffn",
    )(h2, h2, x1, w_up, w_up, conv_w, conv_w, conv_b, conv_b, w_down, g_final)


def _rope_tables(seq):
    inv = ROPE_THETA ** (-jnp.arange(HALF, dtype=F32) / HALF)
    ang = jnp.arange(seq, dtype=F32)[:, None] * inv[None, :]
    cos, sin = jnp.cos(ang), jnp.sin(ang)
    ck = jnp.concatenate([cos, cos, cos, cos], axis=1)
    sk = jnp.concatenate([-sin, sin, -sin, sin], axis=1)
    return (cos.T * SCALE), (sin.T * SCALE), ck, sk


def kernel(x, g_mix, w_in, b_gate, w_branch_a, w_branch_b, w_out, g_ffn, w_up, conv_w, conv_b,
           w_down, g_final):
    batch, seq, _ = x.shape
    assert seq % BLK == 0 and seq % FFN_TM == 0 and seq // BLK <= HALF
    assert g_mix.shape[0] == 1
    x2 = x.reshape(batch * seq, D_MODEL)
    cq, sq, ck, sk = _rope_tables(seq)
    for layer in range(1):
        w = w_in[layer]
        col = lambda k: w[:, k * WIDTH:(k + 1) * WIDTH]
        w_row = jnp.concatenate([col(1), col(4)], axis=1).astype(BF16)
        w_t = jnp.concatenate([col(0), col(2), col(3), col(5)], axis=1).T.astype(BF16)
        w_gate = w[:, 6 * WIDTH:].astype(BF16)
        gm = g_mix[layer][None, :]
        qa, ka, va, qb, kb, vb = _proj(x2, gm, w_row, w_t, cq, sq, ck, sk, batch, seq)
        ya = _moba(qa, ka, va, batch, seq).reshape(batch * seq, WIDTH)
        yb = _sb(qb, kb, vb, batch, seq).reshape(batch * seq, WIDTH)
        x1, h2 = _merge(x2, ya, yb, gm, w_gate, b_gate[layer][None, :],
                        w_branch_a[layer].astype(BF16), w_branch_b[layer].astype(BF16),
                        w_out[layer].astype(BF16), g_ffn[layer][None, :])
        x2 = _ffn(h2, x1, w_up[layer].astype(BF16), conv_w[layer], conv_b[layer][None, :],
                  w_down[layer].astype(BF16), g_final[None, :], seq)
    return x2.reshape(batch, seq, D_MODEL)
```

```python
import functools

import numpy as np
import jax
import jax.numpy as jnp
from jax import lax
from jax.experimental import pallas as pl
from jax.experimental.pallas import tpu as pltpu

F32 = jnp.float32
BF16 = jnp.bfloat16

D_MODEL = 1024
HEAD_DIM = 64
HALF = HEAD_DIM // 2
N_HEADS = 8
WIDTH = N_HEADS * HEAD_DIM
BLK = 256
TOPK = 3
ROPE_THETA = 10000.0
D_FF = 2816
RMS_EPS = 1e-6
NEG = -1e30
SCALE = HEAD_DIM ** -0.5

QA_ROWS = 128
VA_ROWS = 80
MOBA_CH = 2
SB_STOP = 115.0

MERGE_TM = 512
FFN_TM = 512
FFN_TF = D_FF // 2
HALO = 16

VMEM_LIMIT = 48 * 1024 * 1024


def _rms(x, g):
    ms = jnp.mean(x * x, axis=-1, keepdims=True)
    return x * lax.rsqrt(ms + RMS_EPS) * g


def _split_bf16(x):
    hi = x.astype(BF16)
    lo = (x - hi.astype(F32)).astype(BF16)
    return hi, lo


def _dot(a, b):
    return jnp.dot(a, b, preferred_element_type=F32)


def _proj_kernel(nb, x_ref, g_ref, wrow_ref, wt_ref, cq_ref, sq_ref, ck_ref, sk_ref,
                 qa_ref, ka_ref, va_ref, qb_ref, kb_ref, vb_ref, km_ref):
    blk = pl.program_id(0) % nb

    @pl.when(pl.program_id(0) == 0)
    def _():
        km_ref[...] = jnp.zeros_like(km_ref)

    h = _rms(x_ref[...], g_ref[...]).astype(BF16)
    rows = _dot(h, wrow_ref[...])
    cols = lax.dot_general(wt_ref[...], h, (((1,), (1,)), ((), ())),
                           preferred_element_type=F32)

    ka = rows[:, :WIDTH]
    lane = lax.broadcasted_iota(jnp.int32, (BLK, WIDTH), 1)
    first_half = (lane % HEAD_DIM) < HALF
    swapped = jnp.where(first_half, pltpu.roll(ka, WIDTH - HALF, 1), pltpu.roll(ka, HALF, 1))
    ck = jnp.concatenate([ck_ref[...]] * (WIDTH // 128), axis=1)
    sk = jnp.concatenate([sk_ref[...]] * (WIDTH // 128), axis=1)
    ka = ka * ck + swapped * sk
    km_ref[pl.ds(blk, 1), :] = jnp.mean(ka, axis=0, keepdims=True)
    onehot = (lax.broadcasted_iota(jnp.int32, (BLK, HEAD_DIM), 1) == blk).astype(F32)
    ka_blocks = []
    for hd in range(N_HEADS):
        ka_blocks += [ka[:, hd * HEAD_DIM:(hd + 1) * HEAD_DIM], onehot]
    ka_ref[0, 0] = jnp.concatenate(ka_blocks, axis=1).astype(BF16)
    kb_ref[0, 0] = rows[:, WIDTH:].astype(BF16)

    cq = cq_ref[...]
    sq = sq_ref[...]
    km = km_ref[...]
    n_iota = lax.broadcasted_iota(jnp.int32, (nb, BLK), 0).astype(F32)
    blk_f = blk.astype(F32)
    past = n_iota < blk_f
    zeros_q = jnp.zeros((QA_ROWS - HEAD_DIM - nb, BLK), F32)
    zeros_h = jnp.zeros((HEAD_DIM, BLK), F32)
    ones_v = jnp.ones((VA_ROWS - HEAD_DIM, BLK), F32)
    for hd in range(N_HEADS):
        r0 = hd * HEAD_DIM
        t1 = cols[r0:r0 + HALF]
        t2 = cols[r0 + HALF:r0 + HEAD_DIM]
        q = jnp.concatenate([t1 * cq - t2 * sq, t2 * cq + t1 * sq], axis=0)

        q_hi, q_lo = _split_bf16(q)
        km_hi, km_lo = _split_bf16(km[:, r0:r0 + HEAD_DIM])
        gate = _dot(km_hi, q_hi) + (_dot(km_hi, q_lo) + _dot(km_lo, q_hi))
        gate = jnp.where(past, gate, NEG)
        bias = jnp.where(n_iota == blk_f, 0.0, NEG)
        for r in range(TOPK):
            m = jnp.max(gate, axis=0, keepdims=True)
            idx = jnp.min(jnp.where(gate == m, n_iota, float(nb)), axis=0, keepdims=True)
            pick = n_iota == idx
            bias = jnp.where(pick, jnp.where(blk > r, 0.0, bias), bias)
            gate = jnp.where(pick, -jnp.inf, gate)
        qa_ref[0, hd] = jnp.concatenate([q, bias, zeros_q], axis=0).astype(BF16)

        va_ref[0, hd, 0] = jnp.concatenate([cols[WIDTH + r0:WIDTH + r0 + HEAD_DIM], ones_v],
                                           axis=0).astype(BF16)
        qb = cols[2 * WIDTH + r0:2 * WIDTH + r0 + HEAD_DIM] * SCALE
        qb_ref[0, hd] = jnp.concatenate([qb, zeros_h] if hd % 2 == 0 else [zeros_h, qb],
                                        axis=0).astype(BF16)
        vb_ref[0, hd, 0] = cols[3 * WIDTH + r0:3 * WIDTH + r0 + HEAD_DIM].astype(BF16)


def _proj(x2, g_mix, w_row, w_t, cq, sq, ck, sk, batch, seq):
    nb = seq // BLK
    grid = (batch * nb,)
    bi = lambda i: i // nb
    ji = lambda i: i % nb
    out_shape = (
        jax.ShapeDtypeStruct((batch, N_HEADS, QA_ROWS, seq), BF16),
        jax.ShapeDtypeStruct((batch, nb, BLK, N_HEADS * 128), BF16),
        jax.ShapeDtypeStruct((batch, N_HEADS, nb // MOBA_CH, VA_ROWS, MOBA_CH * BLK), BF16),
        jax.ShapeDtypeStruct((batch, N_HEADS, 2 * HEAD_DIM, seq), BF16),
        jax.ShapeDtypeStruct((batch, nb, BLK, WIDTH), BF16),
        jax.ShapeDtypeStruct((batch, N_HEADS, nb, HEAD_DIM, BLK), BF16),
    )
    in_specs = [
        pl.BlockSpec((BLK, D_MODEL), lambda i: (i, 0)),
        pl.BlockSpec((1, D_MODEL), lambda i: (0, 0)),
        pl.BlockSpec((D_MODEL, 2 * WIDTH), lambda i: (0, 0)),
        pl.BlockSpec((4 * WIDTH, D_MODEL), lambda i: (0, 0)),
        pl.BlockSpec((HALF, BLK), lambda i: (0, ji(i))),
        pl.BlockSpec((HALF, BLK), lambda i: (0, ji(i))),
        pl.BlockSpec((BLK, 128), lambda i: (ji(i), 0)),
        pl.BlockSpec((BLK, 128), lambda i: (ji(i), 0)),
    ]
    out_specs = (
        pl.BlockSpec((1, N_HEADS, QA_ROWS, BLK), lambda i: (bi(i), 0, 0, ji(i))),
        pl.BlockSpec((1, 1, BLK, N_HEADS * 128), lambda i: (bi(i), ji(i), 0, 0)),
        pl.BlockSpec((1, N_HEADS, 1, VA_ROWS, BLK),
                     lambda i: (bi(i), 0, ji(i) // MOBA_CH, 0, ji(i) % MOBA_CH)),
        pl.BlockSpec((1, N_HEADS, 2 * HEAD_DIM, BLK), lambda i: (bi(i), 0, 0, ji(i))),
        pl.BlockSpec((1, 1, BLK, WIDTH), lambda i: (bi(i), ji(i), 0, 0)),
        pl.BlockSpec((1, N_HEADS, 1, HEAD_DIM, BLK), lambda i: (bi(i), 0, ji(i), 0, 0)),
    )
    return pl.pallas_call(
        functools.partial(_proj_kernel, nb),
        grid=grid, in_specs=in_specs, out_specs=out_specs, out_shape=out_shape,
        scratch_shapes=[pltpu.VMEM((nb, WIDTH), F32)],
        compiler_params=pltpu.CompilerParams(dimension_semantics=("arbitrary",),
                                             vmem_limit_bytes=VMEM_LIMIT),
        name="proj",
    )(x2, g_mix, w_row, w_t, cq, sq, ck, sk)


def _moba_kernel(q_ref, k_ref, v_ref, o_ref, s_ref, m_ref, acc_ref):
    i = pl.program_id(2)
    n_past = (i + MOBA_CH) // MOBA_CH - 1
    rows = MOBA_CH * BLK
    q_pos = i * BLK + lax.broadcasted_iota(jnp.int32, (rows, BLK), 1)
    k_off = lax.broadcasted_iota(jnp.int32, (rows, BLK), 0)

    def scores(c, slot):
        for hh in range(2):
            for u in range(MOBA_CH):
                s_ref[slot, hh, u * BLK:(u + 1) * BLK, :] = _dot(
                    k_ref[0, c * MOBA_CH + u, :, hh * 128:(hh + 1) * 128], q_ref[0, hh])

    def attend(c, slot, causal):
        for hh in range(2):
            s = s_ref[slot, hh]
            if causal:
                s = jnp.where(c * rows + k_off <= q_pos, s, NEG)
            m = m_ref[hh]
            m_new = jnp.maximum(m, jnp.max(s, axis=0, keepdims=True))
            p = jnp.exp(s - m_new).astype(BF16)
            acc_ref[hh] = acc_ref[hh] * jnp.exp(m - m_new) + _dot(v_ref[0, hh, c], p)
            m_ref[hh] = m_new

    m_ref[...] = jnp.full_like(m_ref, NEG)
    acc_ref[...] = jnp.zeros_like(acc_ref)
    scores(0, 0)

    def body(j, carry):
        scores(2 * j + 1, 1)
        attend(2 * j, 0, False)
        scores(2 * j + 2, 0)
        attend(2 * j + 1, 1, False)
        return carry

    lax.fori_loop(0, n_past // 2, body, 0)

    @pl.when(n_past % 2 == 0)
    def _():
        attend(n_past, 0, True)

    @pl.when(n_past % 2 == 1)
    def _():
        scores(n_past, 1)
        attend(n_past - 1, 0, False)
        attend(n_past, 1, True)

    outs = []
    for hh in range(2):
        acc = acc_ref[hh]
        outs.append((acc[:HEAD_DIM] / acc[HEAD_DIM:HEAD_DIM + 1]).T)
    o_ref[0] = jnp.concatenate(outs, axis=1).astype(BF16)


def _moba(qa, ka, va, batch, seq):
    nb = seq // BLK
    nc = nb // MOBA_CH
    return pl.pallas_call(
        _moba_kernel,
        grid=(batch, N_HEADS // 2, nb),
        in_specs=[
            pl.BlockSpec((1, 2, QA_ROWS, BLK), lambda b, p, i: (b, p, 0, i)),
            pl.BlockSpec((1, nb, BLK, 256), lambda b, p, i: (b, 0, 0, p)),
            pl.BlockSpec((1, 2, nc, VA_ROWS, MOBA_CH * BLK), lambda b, p, i: (b, p, 0, 0, 0)),
        ],
        out_specs=pl.BlockSpec((1, BLK, 128), lambda b, p, i: (b, i, p)),
        out_shape=jax.ShapeDtypeStruct((batch, seq, WIDTH), BF16),
        scratch_shapes=[pltpu.VMEM((2, 2, MOBA_CH * BLK, BLK), F32),
                        pltpu.VMEM((2, 1, BLK), F32),
                        pltpu.VMEM((2, VA_ROWS, BLK), F32)],
        compiler_params=pltpu.CompilerParams(
            dimension_semantics=("arbitrary", "arbitrary", "arbitrary"),
            vmem_limit_bytes=VMEM_LIMIT),
        name="moba",
    )(qa, ka, va)


def _sb_kernel(q_ref, k_ref, v_ref, o_ref, run_ref, acc_ref):
    i = pl.program_id(2)
    k_iota = lax.broadcasted_iota(jnp.int32, (BLK, BLK), 0)
    q_iota = lax.broadcasted_iota(jnp.int32, (BLK, BLK), 1)
    strict = k_iota < q_iota
    later = (q_iota > k_iota).astype(BF16)
    later2 = jnp.concatenate([later, later], axis=1)

    def pair_step(j, diag):
        has_old = j >= 1
        tiles = (j, jnp.maximum(j - 1, 0))
        z, sp, w = {}, {}, {}
        for hh in range(2):
            for u in range(2):
                z[hh, u] = _dot(k_ref[0, tiles[u]], q_ref[0, hh])
        for hh in range(2):
            for u in range(2):
                zz = z[hh, u]
                s = jnp.maximum(zz, 0.0) + jnp.log(1.0 + jnp.exp(-jnp.abs(zz)))
                if diag and u == 0:
                    s = jnp.where(strict, s, 0.0)
                sp[hh, u] = s
                w[hh, u] = jnp.concatenate(_split_bf16(s), axis=0)
        aft = {key: _dot(later2, w[key]) for key in w}
        for hh in range(2):
            run = run_ref[hh]
            pv = None
            for u in range(2):
                if u == 1:
                    run = jnp.where(has_old, run, NEG)
                a = jnp.exp((z[hh, u] - sp[hh, u]) - aft[hh, u] + run)
                if diag and u == 0:
                    a = jnp.where(strict, a, 0.0)
                part = _dot(v_ref[0, hh, tiles[u]], a.astype(BF16))
                pv = part if pv is None else pv + part
                run = run - (aft[hh, u][0:1] + sp[hh, u][0:1])
            acc_ref[hh] += pv
            run_ref[hh] = run

    run_ref[...] = jnp.zeros_like(run_ref)
    acc_ref[...] = jnp.zeros_like(acc_ref)
    pair_step(i, True)

    def cond(j):
        return jnp.logical_and(j >= 0, jnp.max(run_ref[...]) > -SB_STOP)

    def body(j):
        pair_step(j, False)
        return j - 2

    lax.while_loop(cond, body, i - 2)
    o_ref[0] = jnp.concatenate([acc_ref[0].T, acc_ref[1].T], axis=1).astype(BF16)


def _sb(qb, kb, vb, batch, seq):
    nb = seq // BLK
    return pl.pallas_call(
        _sb_kernel,
        grid=(batch, N_HEADS // 2, nb),
        in_specs=[
            pl.BlockSpec((1, 2, 2 * HEAD_DIM, BLK), lambda b, p, i: (b, p, 0, i)),
            pl.BlockSpec((1, nb, BLK, 128), lambda b, p, i: (b, 0, 0, p)),
            pl.BlockSpec((1, 2, nb, HEAD_DIM, BLK), lambda b, p, i: (b, p, 0, 0, 0)),
        ],
        out_specs=pl.BlockSpec((1, BLK, 128), lambda b, p, i: (b, i, p)),
        out_shape=jax.ShapeDtypeStruct((batch, seq, WIDTH), BF16),
        scratch_shapes=[pltpu.VMEM((2, 1, BLK), F32),
                        pltpu.VMEM((2, HEAD_DIM, BLK), F32)],
        compiler_params=pltpu.CompilerParams(
            dimension_semantics=("arbitrary", "arbitrary", "arbitrary"),
            vmem_limit_bytes=VMEM_LIMIT),
        name="sb",
    )(qb, kb, vb)


def _merge_kernel(x_ref, ya_ref, yb_ref, gm_ref, wg_ref, bg_ref, wa_ref, wb_ref, wo_ref, gf_ref,
                  x1_ref, h2_ref):
    x = x_ref[...]
    h = _rms(x, gm_ref[...]).astype(BF16)
    gates = jax.nn.sigmoid(_dot(h, wg_ref[...]) + bg_ref[...])
    merged = (gates[:, :D_MODEL] * _dot(ya_ref[...], wa_ref[...])
              + gates[:, D_MODEL:] * _dot(yb_ref[...], wb_ref[...]))
    x1 = x + _dot(merged.astype(BF16), wo_ref[...])
    x1_ref[...] = x1
    h2_ref[...] = _rms(x1, gf_ref[...]).astype(BF16)


def _merge(x2, ya, yb, g_mix, w_gate, b_gate, w_a, w_b, w_out, g_ffn):
    t = x2.shape[0]
    tm = MERGE_TM
    full = lambda shape: pl.BlockSpec(shape, lambda i: (0, 0))
    return pl.pallas_call(
        _merge_kernel,
        grid=(t // tm,),
        in_specs=[
            pl.BlockSpec((tm, D_MODEL), lambda i: (i, 0)),
            pl.BlockSpec((tm, WIDTH), lambda i: (i, 0)),
            pl.BlockSpec((tm, WIDTH), lambda i: (i, 0)),
            full((1, D_MODEL)),
            full((D_MODEL, 2 * D_MODEL)),
            full((1, 2 * D_MODEL)),
            full((WIDTH, D_MODEL)),
            full((WIDTH, D_MODEL)),
            full((D_MODEL, D_MODEL)),
            full((1, D_MODEL)),
        ],
        out_specs=(pl.BlockSpec((tm, D_MODEL), lambda i: (i, 0)),
                   pl.BlockSpec((tm, D_MODEL), lambda i: (i, 0))),
        out_shape=(jax.ShapeDtypeStruct((t, D_MODEL), F32),
                   jax.ShapeDtypeStruct((t, D_MODEL), BF16)),
        compiler_params=pltpu.CompilerParams(dimension_semantics=("arbitrary",),
                                             vmem_limit_bytes=VMEM_LIMIT),
        name="merge",
    )(x2, ya, yb, g_mix, w_gate, b_gate, w_a, w_b, w_out, g_ffn)


def _ffn_kernel(tiles_per_seq, h_ref, halo_ref, x1_ref, wug_ref, wuv_ref, cwg_ref, cwv_ref,
                cbg_ref, cbv_ref, wd_ref, gfin_ref, o_ref, acc_ref):
    i = pl.program_id(0)
    c = pl.program_id(1)
    tm = h_ref.shape[0]
    halo = jnp.where(i % tiles_per_seq == 0, jnp.zeros_like(halo_ref[...]), halo_ref[...])
    hx = jnp.concatenate([halo, h_ref[...]], axis=0)

    def conv(w_ref, cw_ref, cb_ref):
        u = _dot(hx, w_ref[...])
        cw = cw_ref[...]
        out = (pltpu.roll(u, 2, 0) * cw[0:1] + pltpu.roll(u, 1, 0) * cw[1:2] + u * cw[2:3])
        return out[HALO:] + cb_ref[...]

    ug = conv(wug_ref, cwg_ref, cbg_ref)
    uv = conv(wuv_ref, cwv_ref, cbv_ref)
    act = (ug * jax.nn.sigmoid(ug) * uv).astype(BF16)
    part = _dot(act, wd_ref[...])

    @pl.when(c == 0)
    def _():
        acc_ref[...] = x1_ref[...] + part

    @pl.when(c > 0)
    def _():
        acc_ref[...] += part

    @pl.when(c == pl.num_programs(1) - 1)
    def _():
        o_ref[...] = _rms(acc_ref[...], gfin_ref[...])


def _ffn(h2, x1, w_up, conv_w, conv_b, w_down, g_final, seq):
    t = h2.shape[0]
    tm, tf = FFN_TM, FFN_TF
    nc = D_FF // tf
    hb = tm // HALO
    return pl.pallas_call(
        functools.partial(_ffn_kernel, seq // tm),
        grid=(t // tm, nc),
        in_specs=[
            pl.BlockSpec((tm, D_MODEL), lambda i, c: (i, 0)),
            pl.BlockSpec((HALO, D_MODEL), lambda i, c: (jnp.maximum(i * hb - 1, 0), 0)),
            pl.BlockSpec((tm, D_MODEL), lambda i, c: (i, 0)),
            pl.BlockSpec((D_MODEL, tf), lambda i, c: (0, c)),
            pl.BlockSpec((D_MODEL, tf), lambda i, c: (0, nc + c)),
            pl.BlockSpec((TOPK, tf), lambda i, c: (0, c)),
            pl.BlockSpec((TOPK, tf), lambda i, c: (0, nc + c)),
            pl.BlockSpec((1, tf), lambda i, c: (0, c)),
            pl.BlockSpec((1, tf), lambda i, c: (0, nc + c)),
            pl.BlockSpec((tf, D_MODEL), lambda i, c: (c, 0)),
            pl.BlockSpec((1, D_MODEL), lambda i, c: (0, 0)),
        ],
        out_specs=pl.BlockSpec((tm, D_MODEL), lambda i, c: (i, 0)),
        out_shape=jax.ShapeDtypeStruct((t, D_MODEL), F32),
        scratch_shapes=[pltpu.VMEM((tm, D_MODEL), F32)],
        compiler_params=pltpu.CompilerParams(dimension_semantics=("arbitrary", "arbitrary"),
                                             vmem_limit_bytes=VMEM_LIMIT),
        name="ffn",
    )(h2, h2, x1, w_up, w_up, conv_w, conv_w, conv_b, conv_b, w_down, g_final)


def _rope_tables(seq):
    inv = ROPE_THETA ** (-np.arange(HALF, dtype=np.float64) / HALF)
    ang = np.arange(seq, dtype=np.float64)[:, None] * inv[None, :]
    cos, sin = np.cos(ang), np.sin(ang)
    ck = np.concatenate([cos, cos, cos, cos], axis=1)
    sk = np.concatenate([-sin, sin, -sin, sin], axis=1)
    return tuple(jnp.asarray(t, F32) for t in (cos.T * SCALE, sin.T * SCALE, ck, sk))


def kernel(x, g_mix, w_in, b_gate, w_branch_a, w_branch_b, w_out, g_ffn, w_up, conv_w, conv_b,
           w_down, g_final):
    batch, seq, _ = x.shape
    assert seq % BLK == 0 and seq % FFN_TM == 0 and seq // BLK <= HALF
    assert g_mix.shape[0] == 1
    x2 = x.reshape(batch * seq, D_MODEL)
    cq, sq, ck, sk = _rope_tables(seq)
    for layer in range(1):
        w = w_in[layer]
        col = lambda k: w[:, k * WIDTH:(k + 1) * WIDTH]
        w_row = jnp.concatenate([col(1), col(4)], axis=1).astype(BF16)
        w_t = jnp.concatenate([col(0), col(2), col(3), col(5)], axis=1).T.astype(BF16)
        w_gate = w[:, 6 * WIDTH:].astype(BF16)
        gm = g_mix[layer][None, :]
        qa, ka, va, qb, kb, vb = _proj(x2, gm, w_row, w_t, cq, sq, ck, sk, batch, seq)
        ya = _moba(qa, ka, va, batch, seq).reshape(batch * seq, WIDTH)
        yb = _sb(qb, kb, vb, batch, seq).reshape(batch * seq, WIDTH)
        x1, h2 = _merge(x2, ya, yb, gm, w_gate, b_gate[layer][None, :],
                        w_branch_a[layer].astype(BF16), w_branch_b[layer].astype(BF16),
                        w_out[layer].astype(BF16), g_ffn[layer][None, :])
        x2 = _ffn(h2, x1, w_up[layer].astype(BF16), conv_w[layer], conv_b[layer][None, :],
                  w_down[layer].astype(BF16), g_final[None, :], seq)
    return x2.reshape(batch, seq, D_MODEL)
```

```python
import functools

import numpy as np
import jax
import jax.numpy as jnp
from jax import lax
from jax.experimental import pallas as pl
from jax.experimental.pallas import tpu as pltpu

F32 = jnp.float32
BF16 = jnp.bfloat16

D_MODEL = 1024
HEAD_DIM = 64
HALF = HEAD_DIM // 2
N_HEADS = 8
WIDTH = N_HEADS * HEAD_DIM
BLK = 256
TOPK = 3
ROPE_THETA = 10000.0
D_FF = 2816
RMS_EPS = 1e-6
NEG = -1e30
SCALE = HEAD_DIM ** -0.5

QA_ROWS = 128
VA_ROWS = 80
MOBA_CH = 2
MOBA_TQ = MOBA_CH * BLK
SB_STOP = 115.0

MERGE_TM = 512
FFN_TM = 512
FFN_SUB = 256
HALO = 16

VMEM_LIMIT = 48 * 1024 * 1024


def _rms(x, g):
    ms = jnp.mean(x * x, axis=-1, keepdims=True)
    return x * lax.rsqrt(ms + RMS_EPS) * g


def _split_bf16(x):
    hi = x.astype(BF16)
    lo = (x - hi.astype(F32)).astype(BF16)
    return hi, lo


def _dot(a, b):
    return jnp.dot(a, b, preferred_element_type=F32)


def _proj_kernel(nb, x_ref, g_ref, wrow_ref, wt_ref, cq_ref, sq_ref, ck_ref, sk_ref,
                 qa_ref, ka_ref, va_ref, qb_ref, kb_ref, vb_ref, km_ref):
    blk = pl.program_id(0) % nb

    @pl.when(pl.program_id(0) == 0)
    def _():
        km_ref[...] = jnp.zeros_like(km_ref)

    h = _rms(x_ref[...], g_ref[...]).astype(BF16)
    rows = _dot(h, wrow_ref[...])
    cols = lax.dot_general(wt_ref[...], h, (((1,), (1,)), ((), ())),
                           preferred_element_type=F32)

    ka = rows[:, :WIDTH]
    lane = lax.broadcasted_iota(jnp.int32, (BLK, WIDTH), 1)
    first_half = (lane % HEAD_DIM) < HALF
    swapped = jnp.where(first_half, pltpu.roll(ka, WIDTH - HALF, 1), pltpu.roll(ka, HALF, 1))
    ck = jnp.concatenate([ck_ref[...]] * (WIDTH // 128), axis=1)
    sk = jnp.concatenate([sk_ref[...]] * (WIDTH // 128), axis=1)
    ka = ka * ck + swapped * sk
    km_ref[pl.ds(blk, 1), :] = jnp.mean(ka, axis=0, keepdims=True)
    onehot = (lax.broadcasted_iota(jnp.int32, (BLK, HEAD_DIM), 1) == blk).astype(F32)
    ka_blocks = []
    for hd in range(N_HEADS):
        ka_blocks += [ka[:, hd * HEAD_DIM:(hd + 1) * HEAD_DIM], onehot]
    ka_ref[0, 0] = jnp.concatenate(ka_blocks, axis=1).astype(BF16)
    kb_ref[0, 0] = rows[:, WIDTH:].astype(BF16)

    cq = cq_ref[...]
    sq = sq_ref[...]
    km = km_ref[...]
    n_iota = lax.broadcasted_iota(jnp.int32, (nb, BLK), 0).astype(F32)
    blk_f = blk.astype(F32)
    past = n_iota < blk_f
    zeros_q = jnp.zeros((QA_ROWS - HEAD_DIM - nb, BLK), F32)
    zeros_h = jnp.zeros((HEAD_DIM, BLK), F32)
    ones_v = jnp.ones((VA_ROWS - HEAD_DIM, BLK), F32)
    for hd in range(N_HEADS):
        r0 = hd * HEAD_DIM
        t1 = cols[r0:r0 + HALF]
        t2 = cols[r0 + HALF:r0 + HEAD_DIM]
        q = jnp.concatenate([t1 * cq - t2 * sq, t2 * cq + t1 * sq], axis=0)

        q_hi, q_lo = _split_bf16(q)
        km_hi, km_lo = _split_bf16(km[:, r0:r0 + HEAD_DIM])
        gate = _dot(km_hi, q_hi) + (_dot(km_hi, q_lo) + _dot(km_lo, q_hi))
        gate = jnp.where(past, gate, NEG)
        bias = jnp.where(n_iota == blk_f, 0.0, NEG)
        for r in range(TOPK):
            m = jnp.max(gate, axis=0, keepdims=True)
            idx = jnp.min(jnp.where(gate == m, n_iota, float(nb)), axis=0, keepdims=True)
            pick = n_iota == idx
            bias = jnp.where(pick, jnp.where(blk > r, 0.0, bias), bias)
            gate = jnp.where(pick, -jnp.inf, gate)
        qa_ref[0, hd] = jnp.concatenate([q, bias, zeros_q], axis=0).astype(BF16)

        va_ref[0, hd, 0] = jnp.concatenate([cols[WIDTH + r0:WIDTH + r0 + HEAD_DIM], ones_v],
                                           axis=0).astype(BF16)
        qb = cols[2 * WIDTH + r0:2 * WIDTH + r0 + HEAD_DIM] * SCALE
        qb_ref[0, hd] = jnp.concatenate([qb, zeros_h] if hd % 2 == 0 else [zeros_h, qb],
                                        axis=0).astype(BF16)
        vb_ref[0, hd, 0] = cols[3 * WIDTH + r0:3 * WIDTH + r0 + HEAD_DIM].astype(BF16)


def _proj(x2, g_mix, w_row, w_t, cq, sq, ck, sk, batch, seq):
    nb = seq // BLK
    grid = (batch * nb,)
    bi = lambda i: i // nb
    ji = lambda i: i % nb
    out_shape = (
        jax.ShapeDtypeStruct((batch, N_HEADS, QA_ROWS, seq), BF16),
        jax.ShapeDtypeStruct((batch, nb, BLK, N_HEADS * 128), BF16),
        jax.ShapeDtypeStruct((batch, N_HEADS, nb // MOBA_CH, VA_ROWS, MOBA_CH * BLK), BF16),
        jax.ShapeDtypeStruct((batch, N_HEADS, 2 * HEAD_DIM, seq), BF16),
        jax.ShapeDtypeStruct((batch, nb, BLK, WIDTH), BF16),
        jax.ShapeDtypeStruct((batch, N_HEADS, nb, HEAD_DIM, BLK), BF16),
    )
    in_specs = [
        pl.BlockSpec((BLK, D_MODEL), lambda i: (i, 0)),
        pl.BlockSpec((1, D_MODEL), lambda i: (0, 0)),
        pl.BlockSpec((D_MODEL, 2 * WIDTH), lambda i: (0, 0)),
        pl.BlockSpec((4 * WIDTH, D_MODEL), lambda i: (0, 0)),
        pl.BlockSpec((HALF, BLK), lambda i: (0, ji(i))),
        pl.BlockSpec((HALF, BLK), lambda i: (0, ji(i))),
        pl.BlockSpec((BLK, 128), lambda i: (ji(i), 0)),
        pl.BlockSpec((BLK, 128), lambda i: (ji(i), 0)),
    ]
    out_specs = (
        pl.BlockSpec((1, N_HEADS, QA_ROWS, BLK), lambda i: (bi(i), 0, 0, ji(i))),
        pl.BlockSpec((1, 1, BLK, N_HEADS * 128), lambda i: (bi(i), ji(i), 0, 0)),
        pl.BlockSpec((1, N_HEADS, 1, VA_ROWS, BLK),
                     lambda i: (bi(i), 0, ji(i) // MOBA_CH, 0, ji(i) % MOBA_CH)),
        pl.BlockSpec((1, N_HEADS, 2 * HEAD_DIM, BLK), lambda i: (bi(i), 0, 0, ji(i))),
        pl.BlockSpec((1, 1, BLK, WIDTH), lambda i: (bi(i), ji(i), 0, 0)),
        pl.BlockSpec((1, N_HEADS, 1, HEAD_DIM, BLK), lambda i: (bi(i), 0, ji(i), 0, 0)),
    )
    return pl.pallas_call(
        functools.partial(_proj_kernel, nb),
        grid=grid, in_specs=in_specs, out_specs=out_specs, out_shape=out_shape,
        scratch_shapes=[pltpu.VMEM((nb, WIDTH), F32)],
        compiler_params=pltpu.CompilerParams(dimension_semantics=("arbitrary",),
                                             vmem_limit_bytes=VMEM_LIMIT),
        name="proj",
    )(x2, g_mix, w_row, w_t, cq, sq, ck, sk)


def _moba_kernel(q_ref, k_ref, v_ref, o_ref, s_ref, m_ref, acc_ref):
    n_past = pl.program_id(2)
    rows = MOBA_TQ
    q_pos = lax.broadcasted_iota(jnp.int32, (rows, rows), 1)
    k_pos = lax.broadcasted_iota(jnp.int32, (rows, rows), 0)

    def scores(c, slot):
        for hh in range(2):
            for u in range(MOBA_CH):
                s_ref[slot, hh, u * BLK:(u + 1) * BLK, :] = _dot(
                    k_ref[0, c * MOBA_CH + u, :, hh * 128:(hh + 1) * 128], q_ref[0, hh])

    def attend(c, slot, own):
        for hh in range(2):
            s = s_ref[slot, hh]
            if own:
                s = jnp.where(k_pos <= q_pos, s, NEG)
            m = m_ref[hh]
            m_new = jnp.maximum(m, jnp.max(s, axis=0, keepdims=True))
            p = jnp.exp(s - m_new).astype(BF16)
            acc_ref[hh] = acc_ref[hh] * jnp.exp(m - m_new) + _dot(v_ref[0, hh, c], p)
            m_ref[hh] = m_new

    m_ref[...] = jnp.full_like(m_ref, NEG)
    acc_ref[...] = jnp.zeros_like(acc_ref)
    scores(0, 0)

    def body(j, carry):
        scores(2 * j + 1, 1)
        attend(2 * j, 0, False)
        scores(2 * j + 2, 0)
        attend(2 * j + 1, 1, False)
        return carry

    lax.fori_loop(0, n_past // 2, body, 0)

    @pl.when(n_past % 2 == 0)
    def _():
        attend(n_past, 0, True)

    @pl.when(n_past % 2 == 1)
    def _():
        scores(n_past, 1)
        attend(n_past - 1, 0, False)
        attend(n_past, 1, True)

    outs = []
    for hh in range(2):
        acc = acc_ref[hh]
        outs.append((acc[:HEAD_DIM] / acc[HEAD_DIM:HEAD_DIM + 1]).T)
    o_ref[0] = jnp.concatenate(outs, axis=1).astype(BF16)


def _moba(qa, ka, va, batch, seq):
    nb = seq // BLK
    nc = seq // MOBA_TQ
    return pl.pallas_call(
        _moba_kernel,
        grid=(batch, N_HEADS // 2, nc),
        in_specs=[
            pl.BlockSpec((1, 2, QA_ROWS, MOBA_TQ), lambda b, p, i: (b, p, 0, i)),
            pl.BlockSpec((1, nb, BLK, 256), lambda b, p, i: (b, 0, 0, p)),
            pl.BlockSpec((1, 2, nc, VA_ROWS, MOBA_TQ), lambda b, p, i: (b, p, 0, 0, 0)),
        ],
        out_specs=pl.BlockSpec((1, MOBA_TQ, 128), lambda b, p, i: (b, i, p)),
        out_shape=jax.ShapeDtypeStruct((batch, seq, WIDTH), BF16),
        scratch_shapes=[pltpu.VMEM((2, 2, MOBA_TQ, MOBA_TQ), F32),
                        pltpu.VMEM((2, 1, MOBA_TQ), F32),
                        pltpu.VMEM((2, VA_ROWS, MOBA_TQ), F32)],
        compiler_params=pltpu.CompilerParams(
            dimension_semantics=("arbitrary", "arbitrary", "arbitrary"),
            vmem_limit_bytes=VMEM_LIMIT),
        name="moba",
    )(qa, ka, va)


def _sb_kernel(q_ref, k_ref, v_ref, o_ref, run_ref, acc_ref):
    i = pl.program_id(2)
    k_iota = lax.broadcasted_iota(jnp.int32, (BLK, BLK), 0)
    q_iota = lax.broadcasted_iota(jnp.int32, (BLK, BLK), 1)
    strict = k_iota < q_iota
    later = (q_iota > k_iota).astype(BF16)
    later2 = jnp.concatenate([later, later], axis=1)

    def pair_step(j, diag):
        has_old = j >= 1
        tiles = (j, jnp.maximum(j - 1, 0))
        z, sp, w = {}, {}, {}
        for hh in range(2):
            for u in range(2):
                z[hh, u] = _dot(k_ref[0, tiles[u]], q_ref[0, hh])
        for hh in range(2):
            for u in range(2):
                zz = z[hh, u]
                s = jnp.maximum(zz, 0.0) + jnp.log(1.0 + jnp.exp(-jnp.abs(zz)))
                if diag and u == 0:
                    s = jnp.where(strict, s, 0.0)
                sp[hh, u] = s
                w[hh, u] = jnp.concatenate(_split_bf16(s), axis=0)
        aft = {key: _dot(later2, w[key]) for key in w}
        for hh in range(2):
            run = run_ref[hh]
            pv = None
            for u in range(2):
                if u == 1:
                    run = jnp.where(has_old, run, NEG)
                a = jnp.exp((z[hh, u] - sp[hh, u]) - aft[hh, u] + run)
                if diag and u == 0:
                    a = jnp.where(strict, a, 0.0)
                part = _dot(v_ref[0, hh, tiles[u]], a.astype(BF16))
                pv = part if pv is None else pv + part
                run = run - (aft[hh, u][0:1] + sp[hh, u][0:1])
            acc_ref[hh] += pv
            run_ref[hh] = run

    run_ref[...] = jnp.zeros_like(run_ref)
    acc_ref[...] = jnp.zeros_like(acc_ref)
    pair_step(i, True)

    def cond(j):
        return jnp.logical_and(j >= 0, jnp.max(run_ref[...]) > -SB_STOP)

    def body(j):
        pair_step(j, False)
        return j - 2

    lax.while_loop(cond, body, i - 2)
    o_ref[0] = jnp.concatenate([acc_ref[0].T, acc_ref[1].T], axis=1).astype(BF16)


def _sb(qb, kb, vb, batch, seq):
    nb = seq // BLK
    return pl.pallas_call(
        _sb_kernel,
        grid=(batch, N_HEADS // 2, nb),
        in_specs=[
            pl.BlockSpec((1, 2, 2 * HEAD_DIM, BLK), lambda b, p, i: (b, p, 0, i)),
            pl.BlockSpec((1, nb, BLK, 128), lambda b, p, i: (b, 0, 0, p)),
            pl.BlockSpec((1, 2, nb, HEAD_DIM, BLK), lambda b, p, i: (b, p, 0, 0, 0)),
        ],
        out_specs=pl.BlockSpec((1, BLK, 128), lambda b, p, i: (b, i, p)),
        out_shape=jax.ShapeDtypeStruct((batch, seq, WIDTH), BF16),
        scratch_shapes=[pltpu.VMEM((2, 1, BLK), F32),
                        pltpu.VMEM((2, HEAD_DIM, BLK), F32)],
        compiler_params=pltpu.CompilerParams(
            dimension_semantics=("arbitrary", "arbitrary", "arbitrary"),
            vmem_limit_bytes=VMEM_LIMIT),
        name="sb",
    )(qb, kb, vb)


def _merge_kernel(x_ref, ya_ref, yb_ref, gm_ref, wg_ref, bg_ref, wa_ref, wb_ref, wo_ref, gf_ref,
                  x1_ref, h2_ref):
    x = x_ref[...]
    h = _rms(x, gm_ref[...]).astype(BF16)
    gates = jax.nn.sigmoid(_dot(h, wg_ref[...]) + bg_ref[...])
    merged = (gates[:, :D_MODEL] * _dot(ya_ref[...], wa_ref[...])
              + gates[:, D_MODEL:] * _dot(yb_ref[...], wb_ref[...]))
    x1 = x + _dot(merged.astype(BF16), wo_ref[...])
    x1_ref[...] = x1
    h2_ref[...] = _rms(x1, gf_ref[...]).astype(BF16)


def _merge(x2, ya, yb, g_mix, w_gate, b_gate, w_a, w_b, w_out, g_ffn):
    t = x2.shape[0]
    tm = MERGE_TM
    full = lambda shape: pl.BlockSpec(shape, lambda i: (0, 0))
    return pl.pallas_call(
        _merge_kernel,
        grid=(t // tm,),
        in_specs=[
            pl.BlockSpec((tm, D_MODEL), lambda i: (i, 0)),
            pl.BlockSpec((tm, WIDTH), lambda i: (i, 0)),
            pl.BlockSpec((tm, WIDTH), lambda i: (i, 0)),
            full((1, D_MODEL)),
            full((D_MODEL, 2 * D_MODEL)),
            full((1, 2 * D_MODEL)),
            full((WIDTH, D_MODEL)),
            full((WIDTH, D_MODEL)),
            full((D_MODEL, D_MODEL)),
            full((1, D_MODEL)),
        ],
        out_specs=(pl.BlockSpec((tm, D_MODEL), lambda i: (i, 0)),
                   pl.BlockSpec((tm, D_MODEL), lambda i: (i, 0))),
        out_shape=(jax.ShapeDtypeStruct((t, D_MODEL), F32),
                   jax.ShapeDtypeStruct((t, D_MODEL), BF16)),
        compiler_params=pltpu.CompilerParams(dimension_semantics=("arbitrary",),
                                             vmem_limit_bytes=VMEM_LIMIT),
        name="merge",
    )(x2, ya, yb, g_mix, w_gate, b_gate, w_a, w_b, w_out, g_ffn)


def _ffn_kernel(tiles_per_seq, h_ref, halo_ref, x1_ref, wu_ref, cw_ref, cb_ref, wd_ref, gfin_ref,
                o_ref, u_ref, act_ref):
    i = pl.program_id(0)
    tm = h_ref.shape[0]
    halo = jnp.where(i % tiles_per_seq == 0, jnp.zeros_like(halo_ref[...]), halo_ref[...])
    hx = jnp.concatenate([halo, h_ref[...]], axis=0)

    for s in range(D_FF // FFN_SUB):
        conv = []
        for half in range(2):
            col = half * D_FF + s * FFN_SUB
            slot = 2 * (s % 2) + half
            u_ref[slot] = _dot(hx, wu_ref[:, col:col + FFN_SUB])
            cw = cw_ref[:, col:col + FFN_SUB]
            conv.append(u_ref[slot, HALO - 2:HALO - 2 + tm] * cw[0:1]
                        + u_ref[slot, HALO - 1:HALO - 1 + tm] * cw[1:2]
                        + u_ref[slot, HALO:HALO + tm] * cw[2:3]
                        + cb_ref[:, col:col + FFN_SUB])
        ug, uv = conv
        act_ref[:, s * FFN_SUB:(s + 1) * FFN_SUB] = (ug * jax.nn.sigmoid(ug) * uv).astype(BF16)
    y = x1_ref[...] + _dot(act_ref[...], wd_ref[...])
    o_ref[...] = _rms(y, gfin_ref[...])


def _ffn(h2, x1, w_up, conv_w, conv_b, w_down, g_final, seq):
    t = h2.shape[0]
    tm = FFN_TM
    hb = tm // HALO
    resident = lambda shape: pl.BlockSpec(shape, lambda i: (0, 0), pipeline_mode=pl.Buffered(1))
    return pl.pallas_call(
        functools.partial(_ffn_kernel, seq // tm),
        grid=(t // tm,),
        in_specs=[
            pl.BlockSpec((tm, D_MODEL), lambda i: (i, 0)),
            pl.BlockSpec((HALO, D_MODEL), lambda i: (jnp.maximum(i * hb - 1, 0), 0)),
            pl.BlockSpec((tm, D_MODEL), lambda i: (i, 0)),
            resident((D_MODEL, 2 * D_FF)),
            resident((TOPK, 2 * D_FF)),
            resident((1, 2 * D_FF)),
            resident((D_FF, D_MODEL)),
            resident((1, D_MODEL)),
        ],
        out_specs=pl.BlockSpec((tm, D_MODEL), lambda i: (i, 0)),
        out_shape=jax.ShapeDtypeStruct((t, D_MODEL), F32),
        scratch_shapes=[pltpu.VMEM((4, HALO + tm, FFN_SUB), F32),
                        pltpu.VMEM((tm, D_FF), BF16)],
        compiler_params=pltpu.CompilerParams(dimension_semantics=("arbitrary",),
                                             vmem_limit_bytes=VMEM_LIMIT),
        name="ffn",
    )(h2, h2, x1, w_up, conv_w, conv_b, w_down, g_final)


def _rope_tables(seq):
    inv = ROPE_THETA ** (-np.arange(HALF, dtype=np.float64) / HALF)
    ang = np.arange(seq, dtype=np.float64)[:, None] * inv[None, :]
    cos, sin = np.cos(ang), np.sin(ang)
    ck = np.concatenate([cos, cos, cos, cos], axis=1)
    sk = np.concatenate([-sin, sin, -sin, sin], axis=1)
    return tuple(jnp.asarray(t, F32) for t in (cos.T * SCALE, sin.T * SCALE, ck, sk))


def kernel(x, g_mix, w_in, b_gate, w_branch_a, w_branch_b, w_out, g_ffn, w_up, conv_w, conv_b,
           w_down, g_final):
    batch, seq, _ = x.shape
    assert seq % BLK == 0 and seq % FFN_TM == 0 and seq // BLK <= HALF
    assert g_mix.shape[0] == 1
    x2 = x.reshape(batch * seq, D_MODEL)
    cq, sq, ck, sk = _rope_tables(seq)
    for layer in range(1):
        w = w_in[layer]
        col = lambda k: w[:, k * WIDTH:(k + 1) * WIDTH]
        w_row = jnp.concatenate([col(1), col(4)], axis=1).astype(BF16)
        w_t = jnp.concatenate([col(0), col(2), col(3), col(5)], axis=1).T.astype(BF16)
        w_gate = w[:, 6 * WIDTH:].astype(BF16)
        gm = g_mix[layer][None, :]
        qa, ka, va, qb, kb, vb = _proj(x2, gm, w_row, w_t, cq, sq, ck, sk, batch, seq)
        ya = _moba(qa, ka, va, batch, seq).reshape(batch * seq, WIDTH)
        yb = _sb(qb, kb, vb, batch, seq).reshape(batch * seq, WIDTH)
        x1, h2 = _merge(x2, ya, yb, gm, w_gate, b_gate[layer][None, :],
                        w_branch_a[layer].astype(BF16), w_branch_b[layer].astype(BF16),
                        w_out[layer].astype(BF16), g_ffn[layer][None, :])
        x2 = _ffn(h2, x1, w_up[layer].astype(BF16), conv_w[layer], conv_b[layer][None, :],
                  w_down[layer].astype(BF16), g_final[None, :], seq)
    return x2.reshape(batch, seq, D_MODEL)
```

```python
import functools

import numpy as np
import jax
import jax.numpy as jnp
from jax import lax
from jax.experimental import pallas as pl
from jax.experimental.pallas import tpu as pltpu

F32 = jnp.float32
BF16 = jnp.bfloat16

D_MODEL = 1024
HEAD_DIM = 64
HALF = HEAD_DIM // 2
N_HEADS = 8
WIDTH = N_HEADS * HEAD_DIM
BLK = 256
TOPK = 3
ROPE_THETA = 10000.0
D_FF = 2816
RMS_EPS = 1e-6
NEG = -1e30
SCALE = HEAD_DIM ** -0.5

QA_ROWS = 128
VA_ROWS = 80
MOBA_CH = 2
MOBA_TQ = MOBA_CH * BLK
SB_STOP = 115.0
SB_HEADS = 4
LOG2E = float(np.log2(np.e))

MERGE_TM = 512
FFN_TM = 512
FFN_SUB = 256
HALO = 16

VMEM_LIMIT = 48 * 1024 * 1024


def _rms(x, g):
    ms = jnp.mean(x * x, axis=-1, keepdims=True)
    return x * lax.rsqrt(ms + RMS_EPS) * g


def _split_bf16(x):
    hi = x.astype(BF16)
    lo = (x - hi.astype(F32)).astype(BF16)
    return hi, lo


def _dot(a, b):
    return jnp.dot(a, b, preferred_element_type=F32)


def _proj_kernel(nb, x_ref, g_ref, wrow_ref, wt_ref, cq_ref, sq_ref, ck_ref, sk_ref,
                 qa_ref, ka_ref, va_ref, qb_ref, kb_ref, vb_ref, km_ref):
    blk = pl.program_id(0) % nb

    @pl.when(pl.program_id(0) == 0)
    def _():
        km_ref[...] = jnp.zeros_like(km_ref)

    h = _rms(x_ref[...], g_ref[...]).astype(BF16)
    rows = _dot(h, wrow_ref[...])
    cols = lax.dot_general(wt_ref[...], h, (((1,), (1,)), ((), ())),
                           preferred_element_type=F32)

    ka = rows[:, :WIDTH]
    lane = lax.broadcasted_iota(jnp.int32, (BLK, WIDTH), 1)
    first_half = (lane % HEAD_DIM) < HALF
    swapped = jnp.where(first_half, pltpu.roll(ka, WIDTH - HALF, 1), pltpu.roll(ka, HALF, 1))
    ck = jnp.concatenate([ck_ref[...]] * (WIDTH // 128), axis=1)
    sk = jnp.concatenate([sk_ref[...]] * (WIDTH // 128), axis=1)
    ka = ka * ck + swapped * sk
    km_ref[pl.ds(blk, 1), :] = jnp.mean(ka, axis=0, keepdims=True)
    onehot = (lax.broadcasted_iota(jnp.int32, (BLK, HEAD_DIM), 1) == blk).astype(F32)
    ka_blocks = []
    for hd in range(N_HEADS):
        ka_blocks += [ka[:, hd * HEAD_DIM:(hd + 1) * HEAD_DIM], onehot]
    ka_ref[0, 0] = jnp.concatenate(ka_blocks, axis=1).astype(BF16)
    kb_ref[0, 0] = rows[:, WIDTH:].astype(BF16)

    cq = cq_ref[...]
    sq = sq_ref[...]
    km = km_ref[...]
    n_iota = lax.broadcasted_iota(jnp.int32, (nb, BLK), 0).astype(F32)
    blk_f = blk.astype(F32)
    past = n_iota < blk_f
    zeros_q = jnp.zeros((QA_ROWS - HEAD_DIM - nb, BLK), F32)
    zeros_h = jnp.zeros((HEAD_DIM, BLK), F32)
    ones_v = jnp.ones((VA_ROWS - HEAD_DIM, BLK), F32)
    for hd in range(N_HEADS):
        r0 = hd * HEAD_DIM
        t1 = cols[r0:r0 + HALF]
        t2 = cols[r0 + HALF:r0 + HEAD_DIM]
        q = jnp.concatenate([t1 * cq - t2 * sq, t2 * cq + t1 * sq], axis=0)

        q_hi, q_lo = _split_bf16(q)
        km_hi, km_lo = _split_bf16(km[:, r0:r0 + HEAD_DIM])
        gate = _dot(km_hi, q_hi) + (_dot(km_hi, q_lo) + _dot(km_lo, q_hi))
        gate = jnp.where(past, gate, NEG)
        bias = jnp.where(n_iota == blk_f, 0.0, NEG)
        for r in range(TOPK):
            m = jnp.max(gate, axis=0, keepdims=True)
            idx = jnp.min(jnp.where(gate == m, n_iota, float(nb)), axis=0, keepdims=True)
            pick = n_iota == idx
            bias = jnp.where(pick, jnp.where(blk > r, 0.0, bias), bias)
            gate = jnp.where(pick, -jnp.inf, gate)
        qa_ref[0, hd] = jnp.concatenate([q, bias, zeros_q], axis=0).astype(BF16)

        va_ref[0, hd, 0] = jnp.concatenate([cols[WIDTH + r0:WIDTH + r0 + HEAD_DIM], ones_v],
                                           axis=0).astype(BF16)
        qb = cols[2 * WIDTH + r0:2 * WIDTH + r0 + HEAD_DIM] * (SCALE * LOG2E)
        qb_ref[0, hd] = jnp.concatenate([qb, zeros_h] if hd % 2 == 0 else [zeros_h, qb],
                                        axis=0).astype(BF16)
        vb_ref[0, hd, 0] = cols[3 * WIDTH + r0:3 * WIDTH + r0 + HEAD_DIM].astype(BF16)


def _proj(x2, g_mix, w_row, w_t, cq, sq, ck, sk, batch, seq):
    nb = seq // BLK
    grid = (batch * nb,)
    bi = lambda i: i // nb
    ji = lambda i: i % nb
    out_shape = (
        jax.ShapeDtypeStruct((batch, N_HEADS, QA_ROWS, seq), BF16),
        jax.ShapeDtypeStruct((batch, nb, BLK, N_HEADS * 128), BF16),
        jax.ShapeDtypeStruct((batch, N_HEADS, nb // MOBA_CH, VA_ROWS, MOBA_CH * BLK), BF16),
        jax.ShapeDtypeStruct((batch, N_HEADS, 2 * HEAD_DIM, seq), BF16),
        jax.ShapeDtypeStruct((batch, nb, BLK, WIDTH), BF16),
        jax.ShapeDtypeStruct((batch, N_HEADS, nb, HEAD_DIM, BLK), BF16),
    )
    in_specs = [
        pl.BlockSpec((BLK, D_MODEL), lambda i: (i, 0)),
        pl.BlockSpec((1, D_MODEL), lambda i: (0, 0)),
        pl.BlockSpec((D_MODEL, 2 * WIDTH), lambda i: (0, 0)),
        pl.BlockSpec((4 * WIDTH, D_MODEL), lambda i: (0, 0)),
        pl.BlockSpec((HALF, BLK), lambda i: (0, ji(i))),
        pl.BlockSpec((HALF, BLK), lambda i: (0, ji(i))),
        pl.BlockSpec((BLK, 128), lambda i: (ji(i), 0)),
        pl.BlockSpec((BLK, 128), lambda i: (ji(i), 0)),
    ]
    out_specs = (
        pl.BlockSpec((1, N_HEADS, QA_ROWS, BLK), lambda i: (bi(i), 0, 0, ji(i))),
        pl.BlockSpec((1, 1, BLK, N_HEADS * 128), lambda i: (bi(i), ji(i), 0, 0)),
        pl.BlockSpec((1, N_HEADS, 1, VA_ROWS, BLK),
                     lambda i: (bi(i), 0, ji(i) // MOBA_CH, 0, ji(i) % MOBA_CH)),
        pl.BlockSpec((1, N_HEADS, 2 * HEAD_DIM, BLK), lambda i: (bi(i), 0, 0, ji(i))),
        pl.BlockSpec((1, 1, BLK, WIDTH), lambda i: (bi(i), ji(i), 0, 0)),
        pl.BlockSpec((1, N_HEADS, 1, HEAD_DIM, BLK), lambda i: (bi(i), 0, ji(i), 0, 0)),
    )
    return pl.pallas_call(
        functools.partial(_proj_kernel, nb),
        grid=grid, in_specs=in_specs, out_specs=out_specs, out_shape=out_shape,
        scratch_shapes=[pltpu.VMEM((nb, WIDTH), F32)],
        compiler_params=pltpu.CompilerParams(dimension_semantics=("arbitrary",),
                                             vmem_limit_bytes=VMEM_LIMIT),
        name="proj",
    )(x2, g_mix, w_row, w_t, cq, sq, ck, sk)


def _moba_kernel(q_ref, k_ref, v_ref, o_ref, s_ref, m_ref, acc_ref):
    n_past = pl.program_id(2)
    rows = MOBA_TQ
    q_pos = lax.broadcasted_iota(jnp.int32, (rows, rows), 1)
    k_pos = lax.broadcasted_iota(jnp.int32, (rows, rows), 0)

    def scores(c, slot):
        for hh in range(2):
            for u in range(MOBA_CH):
                s_ref[slot, hh, u * BLK:(u + 1) * BLK, :] = _dot(
                    k_ref[0, c * MOBA_CH + u, :, hh * 128:(hh + 1) * 128], q_ref[0, hh])

    def attend(c, slot, own):
        for hh in range(2):
            s = s_ref[slot, hh]
            if own:
                s = jnp.where(k_pos <= q_pos, s, NEG)
            m = m_ref[hh]
            m_new = jnp.maximum(m, jnp.max(s, axis=0, keepdims=True))
            p = jnp.exp2(s - m_new).astype(BF16)
            acc_ref[hh] = acc_ref[hh] * jnp.exp2(m - m_new) + _dot(v_ref[0, hh, c], p)
            m_ref[hh] = m_new

    m_ref[...] = jnp.full_like(m_ref, NEG)
    acc_ref[...] = jnp.zeros_like(acc_ref)
    scores(0, 0)

    def body(j, carry):
        c = 3 * j
        scores(c + 1, 1)
        attend(c, 0, False)
        scores(c + 2, 2)
        attend(c + 1, 1, False)
        scores(c + 3, 0)
        attend(c + 2, 2, False)
        return carry

    n_loop = n_past // 3
    lax.fori_loop(0, n_loop, body, 0)
    first = 3 * n_loop
    left = n_past - first

    @pl.when(left == 0)
    def _():
        attend(n_past, 0, True)

    @pl.when(left == 1)
    def _():
        scores(n_past, 1)
        attend(first, 0, False)
        attend(n_past, 1, True)

    @pl.when(left == 2)
    def _():
        scores(first + 1, 1)
        attend(first, 0, False)
        scores(n_past, 2)
        attend(first + 1, 1, False)
        attend(n_past, 2, True)

    outs = []
    for hh in range(2):
        acc = acc_ref[hh]
        outs.append((acc[:HEAD_DIM] / acc[HEAD_DIM:HEAD_DIM + 1]).T)
    o_ref[0] = jnp.concatenate(outs, axis=1).astype(BF16)


def _moba(qa, ka, va, batch, seq):
    nb = seq // BLK
    nc = seq // MOBA_TQ
    return pl.pallas_call(
        _moba_kernel,
        grid=(batch, N_HEADS // 2, nc),
        in_specs=[
            pl.BlockSpec((1, 2, QA_ROWS, MOBA_TQ), lambda b, p, i: (b, p, 0, i)),
            pl.BlockSpec((1, nb, BLK, 256), lambda b, p, i: (b, 0, 0, p)),
            pl.BlockSpec((1, 2, nc, VA_ROWS, MOBA_TQ), lambda b, p, i: (b, p, 0, 0, 0)),
        ],
        out_specs=pl.BlockSpec((1, MOBA_TQ, 128), lambda b, p, i: (b, i, p)),
        out_shape=jax.ShapeDtypeStruct((batch, seq, WIDTH), BF16),
        scratch_shapes=[pltpu.VMEM((3, 2, MOBA_TQ, MOBA_TQ), F32),
                        pltpu.VMEM((2, 1, MOBA_TQ), F32),
                        pltpu.VMEM((2, VA_ROWS, MOBA_TQ), F32)],
        compiler_params=pltpu.CompilerParams(
            dimension_semantics=("arbitrary", "arbitrary", "arbitrary"),
            vmem_limit_bytes=VMEM_LIMIT),
        name="moba",
    )(qa, ka, va)


def _sb_kernel(q_ref, k_ref, v_ref, o_ref, run_ref, acc_ref):
    i = pl.program_id(2)
    k_iota = lax.broadcasted_iota(jnp.int32, (BLK, BLK), 0)
    q_iota = lax.broadcasted_iota(jnp.int32, (BLK, BLK), 1)
    strict = k_iota < q_iota
    later = (q_iota > k_iota).astype(BF16)
    later2 = jnp.concatenate([later, later], axis=1)

    def pair_step(j, diag):
        has_old = j >= 1
        tiles = (j, jnp.maximum(j - 1, 0))
        heads = range(SB_HEADS)
        z, sp, w = {}, {}, {}
        for hh in heads:
            lanes = slice((hh // 2) * 128, (hh // 2 + 1) * 128)
            for u in range(2):
                z[hh, u] = _dot(k_ref[0, tiles[u], :, lanes], q_ref[0, hh])
        for hh in heads:
            for u in range(2):
                zz = z[hh, u]
                s = jnp.maximum(zz, 0.0) + jnp.log2(1.0 + jnp.exp2(-jnp.abs(zz)))
                if diag and u == 0:
                    s = jnp.where(strict, s, 0.0)
                sp[hh, u] = s
                w[hh, u] = jnp.concatenate(_split_bf16(s), axis=0)
        aft = {key: _dot(later2, w[key]) for key in w}
        for hh in heads:
            run = run_ref[hh]
            pv = None
            for u in range(2):
                if u == 1:
                    run = jnp.where(has_old, run, NEG)
                a = jnp.exp2((z[hh, u] - sp[hh, u]) - aft[hh, u] + run)
                if diag and u == 0:
                    a = jnp.where(strict, a, 0.0)
                part = _dot(v_ref[0, hh, tiles[u]], a.astype(BF16))
                pv = part if pv is None else pv + part
                run = run - (aft[hh, u][0:1] + sp[hh, u][0:1])
            acc_ref[hh] += pv
            run_ref[hh] = run

    run_ref[...] = jnp.zeros_like(run_ref)
    acc_ref[...] = jnp.zeros_like(acc_ref)
    pair_step(i, True)

    def cond(j):
        return jnp.logical_and(j >= 0, jnp.max(run_ref[...]) > -SB_STOP * LOG2E)

    def body(j):
        pair_step(j, False)
        return j - 2

    lax.while_loop(cond, body, i - 2)
    o_ref[0] = jnp.concatenate([acc_ref[hh].T for hh in range(SB_HEADS)], axis=1).astype(BF16)


def _sb(qb, kb, vb, batch, seq):
    nb = seq // BLK
    return pl.pallas_call(
        _sb_kernel,
        grid=(batch, N_HEADS // SB_HEADS, nb),
        in_specs=[
            pl.BlockSpec((1, SB_HEADS, 2 * HEAD_DIM, BLK), lambda b, g, i: (b, g, 0, i)),
            pl.BlockSpec((1, nb, BLK, SB_HEADS * HEAD_DIM), lambda b, g, i: (b, 0, 0, g)),
            pl.BlockSpec((1, SB_HEADS, nb, HEAD_DIM, BLK), lambda b, g, i: (b, g, 0, 0, 0)),
        ],
        out_specs=pl.BlockSpec((1, BLK, SB_HEADS * HEAD_DIM), lambda b, g, i: (b, i, g)),
        out_shape=jax.ShapeDtypeStruct((batch, seq, WIDTH), BF16),
        scratch_shapes=[pltpu.VMEM((SB_HEADS, 1, BLK), F32),
                        pltpu.VMEM((SB_HEADS, HEAD_DIM, BLK), F32)],
        compiler_params=pltpu.CompilerParams(
            dimension_semantics=("arbitrary", "arbitrary", "arbitrary"),
            vmem_limit_bytes=VMEM_LIMIT),
        name="sb",
    )(qb, kb, vb)


def _merge_kernel(x_ref, ya_ref, yb_ref, gm_ref, wg_ref, bg_ref, wa_ref, wb_ref, wo_ref, gf_ref,
                  x1_ref, h2_ref):
    x = x_ref[...]
    h = _rms(x, gm_ref[...]).astype(BF16)
    gates = jax.nn.sigmoid(_dot(h, wg_ref[...]) + bg_ref[...])
    merged = (gates[:, :D_MODEL] * _dot(ya_ref[...], wa_ref[...])
              + gates[:, D_MODEL:] * _dot(yb_ref[...], wb_ref[...]))
    x1 = x + _dot(merged.astype(BF16), wo_ref[...])
    x1_ref[...] = x1
    h2_ref[...] = _rms(x1, gf_ref[...]).astype(BF16)


def _merge(x2, ya, yb, g_mix, w_gate, b_gate, w_a, w_b, w_out, g_ffn):
    t = x2.shape[0]
    tm = MERGE_TM
    full = lambda shape: pl.BlockSpec(shape, lambda i: (0, 0))
    return pl.pallas_call(
        _merge_kernel,
        grid=(t // tm,),
        in_specs=[
            pl.BlockSpec((tm, D_MODEL), lambda i: (i, 0)),
            pl.BlockSpec((tm, WIDTH), lambda i: (i, 0)),
            pl.BlockSpec((tm, WIDTH), lambda i: (i, 0)),
            full((1, D_MODEL)),
            full((D_MODEL, 2 * D_MODEL)),
            full((1, 2 * D_MODEL)),
            full((WIDTH, D_MODEL)),
            full((WIDTH, D_MODEL)),
            full((D_MODEL, D_MODEL)),
            full((1, D_MODEL)),
        ],
        out_specs=(pl.BlockSpec((tm, D_MODEL), lambda i: (i, 0)),
                   pl.BlockSpec((tm, D_MODEL), lambda i: (i, 0))),
        out_shape=(jax.ShapeDtypeStruct((t, D_MODEL), F32),
                   jax.ShapeDtypeStruct((t, D_MODEL), BF16)),
        compiler_params=pltpu.CompilerParams(dimension_semantics=("arbitrary",),
                                             vmem_limit_bytes=VMEM_LIMIT),
        name="merge",
    )(x2, ya, yb, g_mix, w_gate, b_gate, w_a, w_b, w_out, g_ffn)


def _ffn_kernel(tiles_per_seq, h_ref, halo_ref, x1_ref, wu_ref, cw_ref, cb_ref, wd_ref, gfin_ref,
                o_ref, u_ref, act_ref):
    i = pl.program_id(0)
    tm = h_ref.shape[0]
    halo = jnp.where(i % tiles_per_seq == 0, jnp.zeros_like(halo_ref[...]), halo_ref[...])
    hx = jnp.concatenate([halo, h_ref[...]], axis=0)

    for s in range(D_FF // FFN_SUB):
        conv = []
        for half in range(2):
            col = half * D_FF + s * FFN_SUB
            slot = 2 * (s % 2) + half
            u_ref[slot] = _dot(hx, wu_ref[:, col:col + FFN_SUB])
            cw = cw_ref[:, col:col + FFN_SUB]
            conv.append(u_ref[slot, HALO - 2:HALO - 2 + tm] * cw[0:1]
                        + u_ref[slot, HALO - 1:HALO - 1 + tm] * cw[1:2]
                        + u_ref[slot, HALO:HALO + tm] * cw[2:3]
                        + cb_ref[:, col:col + FFN_SUB])
        ug, uv = conv
        act_ref[:, s * FFN_SUB:(s + 1) * FFN_SUB] = (ug * jax.nn.sigmoid(ug) * uv).astype(BF16)
    y = x1_ref[...] + _dot(act_ref[...], wd_ref[...])
    o_ref[...] = _rms(y, gfin_ref[...])


def _ffn(h2, x1, w_up, conv_w, conv_b, w_down, g_final, seq):
    t = h2.shape[0]
    tm = FFN_TM
    hb = tm // HALO
    resident = lambda shape: pl.BlockSpec(shape, lambda i: (0, 0), pipeline_mode=pl.Buffered(1))
    return pl.pallas_call(
        functools.partial(_ffn_kernel, seq // tm),
        grid=(t // tm,),
        in_specs=[
            pl.BlockSpec((tm, D_MODEL), lambda i: (i, 0)),
            pl.BlockSpec((HALO, D_MODEL), lambda i: (jnp.maximum(i * hb - 1, 0), 0)),
            pl.BlockSpec((tm, D_MODEL), lambda i: (i, 0)),
            resident((D_MODEL, 2 * D_FF)),
            resident((TOPK, 2 * D_FF)),
            resident((1, 2 * D_FF)),
            resident((D_FF, D_MODEL)),
            resident((1, D_MODEL)),
        ],
        out_specs=pl.BlockSpec((tm, D_MODEL), lambda i: (i, 0)),
        out_shape=jax.ShapeDtypeStruct((t, D_MODEL), F32),
        scratch_shapes=[pltpu.VMEM((4, HALO + tm, FFN_SUB), F32),
                        pltpu.VMEM((tm, D_FF), BF16)],
        compiler_params=pltpu.CompilerParams(dimension_semantics=("arbitrary",),
                                             vmem_limit_bytes=VMEM_LIMIT),
        name="ffn",
    )(h2, h2, x1, w_up, conv_w, conv_b, w_down, g_final)


def _rope_tables(seq):
    inv = ROPE_THETA ** (-np.arange(HALF, dtype=np.float64) / HALF)
    ang = np.arange(seq, dtype=np.float64)[:, None] * inv[None, :]
    cos, sin = np.cos(ang), np.sin(ang)
    ck = np.concatenate([cos, cos, cos, cos], axis=1)
    sk = np.concatenate([-sin, sin, -sin, sin], axis=1)
    q_scale = SCALE * LOG2E
    return tuple(jnp.asarray(t, F32) for t in (cos.T * q_scale, sin.T * q_scale, ck, sk))


def kernel(x, g_mix, w_in, b_gate, w_branch_a, w_branch_b, w_out, g_ffn, w_up, conv_w, conv_b,
           w_down, g_final):
    batch, seq, _ = x.shape
    assert seq % BLK == 0 and seq % FFN_TM == 0 and seq // BLK <= HALF
    assert g_mix.shape[0] == 1
    x2 = x.reshape(batch * seq, D_MODEL)
    cq, sq, ck, sk = _rope_tables(seq)
    for layer in range(1):
        w = w_in[layer]
        col = lambda k: w[:, k * WIDTH:(k + 1) * WIDTH]
        w_row = jnp.concatenate([col(1), col(4)], axis=1).astype(BF16)
        w_t = jnp.concatenate([col(0), col(2), col(3), col(5)], axis=1).T.astype(BF16)
        w_gate = w[:, 6 * WIDTH:].astype(BF16)
        gm = g_mix[layer][None, :]
        qa, ka, va, qb, kb, vb = _proj(x2, gm, w_row, w_t, cq, sq, ck, sk, batch, seq)
        ya = _moba(qa, ka, va, batch, seq).reshape(batch * seq, WIDTH)
        yb = _sb(qb, kb, vb, batch, seq).reshape(batch * seq, WIDTH)
        x1, h2 = _merge(x2, ya, yb, gm, w_gate, b_gate[layer][None, :],
                        w_branch_a[layer].astype(BF16), w_branch_b[layer].astype(BF16),
                        w_out[layer].astype(BF16), g_ffn[layer][None, :])
        x2 = _ffn(h2, x1, w_up[layer].astype(BF16), conv_w[layer], conv_b[layer][None, :],
                  w_down[layer].astype(BF16), g_final[None, :], seq)
    return x2.reshape(batch, seq, D_MODEL)
```

```python
import functools

import numpy as np
import jax
import jax.numpy as jnp
from jax import lax
from jax.experimental import pallas as pl
from jax.experimental.pallas import tpu as pltpu

F32 = jnp.float32
BF16 = jnp.bfloat16

D_MODEL = 1024
HEAD_DIM = 64
HALF = HEAD_DIM // 2
N_HEADS = 8
WIDTH = N_HEADS * HEAD_DIM
BLK = 256
TOPK = 3
ROPE_THETA = 10000.0
D_FF = 2816
RMS_EPS = 1e-6
NEG = -1e30
SCALE = HEAD_DIM ** -0.5

QA_ROWS = 128
VA_ROWS = 80
MOBA_CH = 2
MOBA_TQ = MOBA_CH * BLK
SB_STOP = 115.0
SB_HEADS = 4
LOG2E = float(np.log2(np.e))

MERGE_TM = 512
FFN_TM = 512
FFN_SUB = 256
HALO = 16

VMEM_LIMIT = 48 * 1024 * 1024


def _rms(x, g):
    ms = jnp.mean(x * x, axis=-1, keepdims=True)
    return x * lax.rsqrt(ms + RMS_EPS) * g


def _split_bf16(x):
    hi = x.astype(BF16)
    lo = (x - hi.astype(F32)).astype(BF16)
    return hi, lo


def _dot(a, b):
    return jnp.dot(a, b, preferred_element_type=F32)


def _dot_t(a_t, b):
    return lax.dot_general(a_t, b, (((0,), (0,)), ((), ())), preferred_element_type=F32)


def _proj_kernel(nb, x_ref, g_ref, wrow_ref, wt_ref, cq_ref, sq_ref, ck_ref, sk_ref,
                 qa_ref, ka_ref, va_ref, qb_ref, kb_ref, vb_ref, km_ref):
    blk = pl.program_id(0) % nb

    @pl.when(pl.program_id(0) == 0)
    def _():
        km_ref[...] = jnp.zeros_like(km_ref)

    h = _rms(x_ref[...], g_ref[...]).astype(BF16)
    rows = _dot(h, wrow_ref[...])
    cols = lax.dot_general(wt_ref[...], h, (((1,), (1,)), ((), ())),
                           preferred_element_type=F32)

    ka = rows[:, :WIDTH]
    lane = lax.broadcasted_iota(jnp.int32, (BLK, WIDTH), 1)
    first_half = (lane % HEAD_DIM) < HALF
    swapped = jnp.where(first_half, pltpu.roll(ka, WIDTH - HALF, 1), pltpu.roll(ka, HALF, 1))
    ck = jnp.concatenate([ck_ref[...]] * (WIDTH // 128), axis=1)
    sk = jnp.concatenate([sk_ref[...]] * (WIDTH // 128), axis=1)
    ka = ka * ck + swapped * sk
    km_ref[pl.ds(blk, 1), :] = jnp.mean(ka, axis=0, keepdims=True)
    onehot = (lax.broadcasted_iota(jnp.int32, (BLK, HEAD_DIM), 1) == blk).astype(F32)
    ka_blocks = []
    for hd in range(N_HEADS):
        ka_blocks += [ka[:, hd * HEAD_DIM:(hd + 1) * HEAD_DIM], onehot]
    ka_ref[0, 0] = jnp.concatenate(ka_blocks, axis=1).astype(BF16)
    kb_ref[0, 0] = rows[:, WIDTH:].astype(BF16)

    cq = cq_ref[...]
    sq = sq_ref[...]
    km = km_ref[...]
    n_iota = lax.broadcasted_iota(jnp.int32, (nb, BLK), 0).astype(F32)
    blk_f = blk.astype(F32)
    past = n_iota < blk_f
    zeros_q = jnp.zeros((QA_ROWS - HEAD_DIM - nb, BLK), F32)
    zeros_h = jnp.zeros((HEAD_DIM, BLK), F32)
    ones_v = jnp.ones((VA_ROWS - HEAD_DIM, BLK), F32)
    for hd in range(N_HEADS):
        r0 = hd * HEAD_DIM
        t1 = cols[r0:r0 + HALF]
        t2 = cols[r0 + HALF:r0 + HEAD_DIM]
        q = jnp.concatenate([t1 * cq - t2 * sq, t2 * cq + t1 * sq], axis=0)

        q_hi, q_lo = _split_bf16(q)
        km_hi, km_lo = _split_bf16(km[:, r0:r0 + HEAD_DIM])
        gate = _dot(km_hi, q_hi) + (_dot(km_hi, q_lo) + _dot(km_lo, q_hi))
        gate = jnp.where(past, gate, NEG)
        bias = jnp.where(n_iota == blk_f, 0.0, NEG)
        for r in range(TOPK):
            m = jnp.max(gate, axis=0, keepdims=True)
            idx = jnp.min(jnp.where(gate == m, n_iota, float(nb)), axis=0, keepdims=True)
            pick = n_iota == idx
            bias = jnp.where(pick, jnp.where(blk > r, 0.0, bias), bias)
            gate = jnp.where(pick, -jnp.inf, gate)
        qa_ref[0, hd] = jnp.concatenate([q, bias, zeros_q], axis=0).astype(BF16)

        va_ref[0, hd, 0] = jnp.concatenate([cols[WIDTH + r0:WIDTH + r0 + HEAD_DIM], ones_v],
                                           axis=0).astype(BF16)
        qb = cols[2 * WIDTH + r0:2 * WIDTH + r0 + HEAD_DIM] * (SCALE * LOG2E)
        qb_ref[0, hd] = jnp.concatenate([qb, zeros_h] if hd % 2 == 0 else [zeros_h, qb],
                                        axis=0).astype(BF16)
        vb_ref[0, hd, 0] = cols[3 * WIDTH + r0:3 * WIDTH + r0 + HEAD_DIM].astype(BF16)


def _proj(x2, g_mix, w_row, w_t, cq, sq, ck, sk, batch, seq):
    nb = seq // BLK
    grid = (batch * nb,)
    bi = lambda i: i // nb
    ji = lambda i: i % nb
    out_shape = (
        jax.ShapeDtypeStruct((batch, N_HEADS, QA_ROWS, seq), BF16),
        jax.ShapeDtypeStruct((batch, nb, BLK, N_HEADS * 128), BF16),
        jax.ShapeDtypeStruct((batch, N_HEADS, nb // MOBA_CH, VA_ROWS, MOBA_CH * BLK), BF16),
        jax.ShapeDtypeStruct((batch, N_HEADS, 2 * HEAD_DIM, seq), BF16),
        jax.ShapeDtypeStruct((batch, nb, BLK, WIDTH), BF16),
        jax.ShapeDtypeStruct((batch, N_HEADS, nb, HEAD_DIM, BLK), BF16),
    )
    in_specs = [
        pl.BlockSpec((BLK, D_MODEL), lambda i: (i, 0)),
        pl.BlockSpec((1, D_MODEL), lambda i: (0, 0)),
        pl.BlockSpec((D_MODEL, 2 * WIDTH), lambda i: (0, 0)),
        pl.BlockSpec((4 * WIDTH, D_MODEL), lambda i: (0, 0)),
        pl.BlockSpec((HALF, BLK), lambda i: (0, ji(i))),
        pl.BlockSpec((HALF, BLK), lambda i: (0, ji(i))),
        pl.BlockSpec((BLK, 128), lambda i: (ji(i), 0)),
        pl.BlockSpec((BLK, 128), lambda i: (ji(i), 0)),
    ]
    out_specs = (
        pl.BlockSpec((1, N_HEADS, QA_ROWS, BLK), lambda i: (bi(i), 0, 0, ji(i))),
        pl.BlockSpec((1, 1, BLK, N_HEADS * 128), lambda i: (bi(i), ji(i), 0, 0)),
        pl.BlockSpec((1, N_HEADS, 1, VA_ROWS, BLK),
                     lambda i: (bi(i), 0, ji(i) // MOBA_CH, 0, ji(i) % MOBA_CH)),
        pl.BlockSpec((1, N_HEADS, 2 * HEAD_DIM, BLK), lambda i: (bi(i), 0, 0, ji(i))),
        pl.BlockSpec((1, 1, BLK, WIDTH), lambda i: (bi(i), ji(i), 0, 0)),
        pl.BlockSpec((1, N_HEADS, 1, HEAD_DIM, BLK), lambda i: (bi(i), 0, ji(i), 0, 0)),
    )
    return pl.pallas_call(
        functools.partial(_proj_kernel, nb),
        grid=grid, in_specs=in_specs, out_specs=out_specs, out_shape=out_shape,
        scratch_shapes=[pltpu.VMEM((nb, WIDTH), F32)],
        compiler_params=pltpu.CompilerParams(dimension_semantics=("arbitrary",),
                                             vmem_limit_bytes=VMEM_LIMIT),
        name="proj",
    )(x2, g_mix, w_row, w_t, cq, sq, ck, sk)


def _moba_kernel(q_ref, k_ref, v_ref, o_ref, s_ref, m_ref, acc_ref):
    n_past = pl.program_id(2)
    rows = MOBA_TQ
    q_pos = lax.broadcasted_iota(jnp.int32, (rows, rows), 1)
    k_pos = lax.broadcasted_iota(jnp.int32, (rows, rows), 0)

    def scores(c, slot):
        for hh in range(2):
            for u in range(MOBA_CH):
                s_ref[slot, hh, u * BLK:(u + 1) * BLK, :] = _dot(
                    k_ref[0, c * MOBA_CH + u, :, hh * 128:(hh + 1) * 128], q_ref[0, hh])

    def attend(c, slot, own):
        for hh in range(2):
            s = s_ref[slot, hh]
            if own:
                s = jnp.where(k_pos <= q_pos, s, NEG)
            m = m_ref[hh]
            m_new = jnp.maximum(m, jnp.max(s, axis=0, keepdims=True))
            p = jnp.exp2(s - m_new).astype(BF16)
            acc_ref[hh] = acc_ref[hh] * jnp.exp2(m - m_new) + _dot(v_ref[0, hh, c], p)
            m_ref[hh] = m_new

    m_ref[...] = jnp.full_like(m_ref, NEG)
    acc_ref[...] = jnp.zeros_like(acc_ref)
    scores(0, 0)

    def body(j, carry):
        c = 3 * j
        scores(c + 1, 1)
        attend(c, 0, False)
        scores(c + 2, 2)
        attend(c + 1, 1, False)
        scores(c + 3, 0)
        attend(c + 2, 2, False)
        return carry

    n_loop = n_past // 3
    lax.fori_loop(0, n_loop, body, 0)
    first = 3 * n_loop
    left = n_past - first

    @pl.when(left == 0)
    def _():
        attend(n_past, 0, True)

    @pl.when(left == 1)
    def _():
        scores(n_past, 1)
        attend(first, 0, False)
        attend(n_past, 1, True)

    @pl.when(left == 2)
    def _():
        scores(first + 1, 1)
        attend(first, 0, False)
        scores(n_past, 2)
        attend(first + 1, 1, False)
        attend(n_past, 2, True)

    outs = []
    for hh in range(2):
        acc = acc_ref[hh]
        outs.append(acc[:HEAD_DIM] / acc[HEAD_DIM:HEAD_DIM + 1])
    o_ref[0] = jnp.concatenate(outs, axis=0).astype(BF16)


def _moba(qa, ka, va, batch, seq):
    nb = seq // BLK
    nc = seq // MOBA_TQ
    return pl.pallas_call(
        _moba_kernel,
        grid=(batch, N_HEADS // 2, nc),
        in_specs=[
            pl.BlockSpec((1, 2, QA_ROWS, MOBA_TQ), lambda b, p, i: (b, p, 0, i)),
            pl.BlockSpec((1, nb, BLK, 256), lambda b, p, i: (b, 0, 0, p)),
            pl.BlockSpec((1, 2, nc, VA_ROWS, MOBA_TQ), lambda b, p, i: (b, p, 0, 0, 0)),
        ],
        out_specs=pl.BlockSpec((1, 2 * HEAD_DIM, MOBA_TQ), lambda b, p, i: (b, p, i)),
        out_shape=jax.ShapeDtypeStruct((batch, WIDTH, seq), BF16),
        scratch_shapes=[pltpu.VMEM((3, 2, MOBA_TQ, MOBA_TQ), F32),
                        pltpu.VMEM((2, 1, MOBA_TQ), F32),
                        pltpu.VMEM((2, VA_ROWS, MOBA_TQ), F32)],
        compiler_params=pltpu.CompilerParams(
            dimension_semantics=("arbitrary", "arbitrary", "arbitrary"),
            vmem_limit_bytes=VMEM_LIMIT),
        name="moba",
    )(qa, ka, va)


def _sb_kernel(q_ref, k_ref, v_ref, o_ref, run_ref, acc_ref):
    i = pl.program_id(2)
    k_iota = lax.broadcasted_iota(jnp.int32, (BLK, BLK), 0)
    q_iota = lax.broadcasted_iota(jnp.int32, (BLK, BLK), 1)
    strict = k_iota < q_iota
    later = (q_iota > k_iota).astype(BF16)
    later2 = jnp.concatenate([later, later], axis=1)

    def pair_step(j, diag):
        has_old = j >= 1
        tiles = (j, jnp.maximum(j - 1, 0))
        heads = range(SB_HEADS)
        z, sp, w = {}, {}, {}
        for hh in heads:
            lanes = slice((hh // 2) * 128, (hh // 2 + 1) * 128)
            for u in range(2):
                z[hh, u] = _dot(k_ref[0, tiles[u], :, lanes], q_ref[0, hh])
        for hh in heads:
            for u in range(2):
                zz = z[hh, u]
                s = jnp.maximum(zz, 0.0) + jnp.log2(1.0 + jnp.exp2(-jnp.abs(zz)))
                if diag and u == 0:
                    s = jnp.where(strict, s, 0.0)
                sp[hh, u] = s
                w[hh, u] = jnp.concatenate(_split_bf16(s), axis=0)
        aft = {key: _dot(later2, w[key]) for key in w}
        for hh in heads:
            run = run_ref[hh]
            pv = None
            for u in range(2):
                if u == 1:
                    run = jnp.where(has_old, run, NEG)
                a = jnp.exp2((z[hh, u] - sp[hh, u]) - aft[hh, u] + run)
                if diag and u == 0:
                    a = jnp.where(strict, a, 0.0)
                part = _dot(v_ref[0, hh, tiles[u]], a.astype(BF16))
                pv = part if pv is None else pv + part
                run = run - (aft[hh, u][0:1] + sp[hh, u][0:1])
            acc_ref[hh] += pv
            run_ref[hh] = run

    run_ref[...] = jnp.zeros_like(run_ref)
    acc_ref[...] = jnp.zeros_like(acc_ref)
    pair_step(i, True)

    def cond(j):
        return jnp.logical_and(j >= 0, jnp.max(run_ref[...]) > -SB_STOP * LOG2E)

    def body(j):
        pair_step(j, False)
        return j - 2

    lax.while_loop(cond, body, i - 2)
    o_ref[0] = jnp.concatenate([acc_ref[hh] for hh in range(SB_HEADS)], axis=0).astype(BF16)


def _sb(qb, kb, vb, batch, seq):
    nb = seq // BLK
    return pl.pallas_call(
        _sb_kernel,
        grid=(batch, N_HEADS // SB_HEADS, nb),
        in_specs=[
            pl.BlockSpec((1, SB_HEADS, 2 * HEAD_DIM, BLK), lambda b, g, i: (b, g, 0, i)),
            pl.BlockSpec((1, nb, BLK, SB_HEADS * HEAD_DIM), lambda b, g, i: (b, 0, 0, g)),
            pl.BlockSpec((1, SB_HEADS, nb, HEAD_DIM, BLK), lambda b, g, i: (b, g, 0, 0, 0)),
        ],
        out_specs=pl.BlockSpec((1, SB_HEADS * HEAD_DIM, BLK), lambda b, g, i: (b, g, i)),
        out_shape=jax.ShapeDtypeStruct((batch, WIDTH, seq), BF16),
        scratch_shapes=[pltpu.VMEM((SB_HEADS, 1, BLK), F32),
                        pltpu.VMEM((SB_HEADS, HEAD_DIM, BLK), F32)],
        compiler_params=pltpu.CompilerParams(
            dimension_semantics=("arbitrary", "arbitrary", "arbitrary"),
            vmem_limit_bytes=VMEM_LIMIT),
        name="sb",
    )(qb, kb, vb)


def _merge_kernel(x_ref, ya_ref, yb_ref, gm_ref, wg_ref, bg_ref, wa_ref, wb_ref, wo_ref, gf_ref,
                  x1_ref, h2_ref):
    x = x_ref[...]
    h = _rms(x, gm_ref[...]).astype(BF16)
    gates = jax.nn.sigmoid(_dot(h, wg_ref[...]) + bg_ref[...])
    merged = (gates[:, :D_MODEL] * _dot_t(ya_ref[0], wa_ref[...])
              + gates[:, D_MODEL:] * _dot_t(yb_ref[0], wb_ref[...]))
    x1 = x + _dot(merged.astype(BF16), wo_ref[...])
    x1_ref[...] = x1
    h2_ref[...] = _rms(x1, gf_ref[...]).astype(BF16)


def _merge(x2, ya_t, yb_t, g_mix, w_gate, b_gate, w_a, w_b, w_out, g_ffn):
    t = x2.shape[0]
    tm = MERGE_TM
    tiles_per_seq = ya_t.shape[2] // tm
    full = lambda shape: pl.BlockSpec(shape, lambda i: (0, 0))
    y_spec = pl.BlockSpec((1, WIDTH, tm), lambda i: (i // tiles_per_seq, 0, i % tiles_per_seq))
    return pl.pallas_call(
        _merge_kernel,
        grid=(t // tm,),
        in_specs=[
            pl.BlockSpec((tm, D_MODEL), lambda i: (i, 0)),
            y_spec,
            y_spec,
            full((1, D_MODEL)),
            full((D_MODEL, 2 * D_MODEL)),
            full((1, 2 * D_MODEL)),
            full((WIDTH, D_MODEL)),
            full((WIDTH, D_MODEL)),
            full((D_MODEL, D_MODEL)),
            full((1, D_MODEL)),
        ],
        out_specs=(pl.BlockSpec((tm, D_MODEL), lambda i: (i, 0)),
                   pl.BlockSpec((tm, D_MODEL), lambda i: (i, 0))),
        out_shape=(jax.ShapeDtypeStruct((t, D_MODEL), F32),
                   jax.ShapeDtypeStruct((t, D_MODEL), BF16)),
        compiler_params=pltpu.CompilerParams(dimension_semantics=("arbitrary",),
                                             vmem_limit_bytes=VMEM_LIMIT),
        name="merge",
    )(x2, ya_t, yb_t, g_mix, w_gate, b_gate, w_a, w_b, w_out, g_ffn)


def _ffn_kernel(tiles_per_seq, h_ref, halo_ref, x1_ref, wu_ref, cw_ref, cb_ref, wd_ref, gfin_ref,
                o_ref, act_ref):
    i = pl.program_id(0)
    halo = jnp.where(i % tiles_per_seq == 0, jnp.zeros_like(halo_ref[...]), halo_ref[...])
    hx = jnp.concatenate([halo, h_ref[...]], axis=0)

    for s in range(D_FF // FFN_SUB):
        conv = []
        for half in range(2):
            col = half * D_FF + s * FFN_SUB
            u = _dot(hx, wu_ref[:, col:col + FFN_SUB])
            cw = cw_ref[:, col:col + FFN_SUB]
            taps = pltpu.roll(u, 2, 0) * cw[0:1] + pltpu.roll(u, 1, 0) * cw[1:2] + u * cw[2:3]
            conv.append(taps[HALO:] + cb_ref[:, col:col + FFN_SUB])
        ug, uv = conv
        act_ref[:, s * FFN_SUB:(s + 1) * FFN_SUB] = (ug * jax.nn.sigmoid(ug) * uv).astype(BF16)
    y = x1_ref[...] + _dot(act_ref[...], wd_ref[...])
    o_ref[...] = _rms(y, gfin_ref[...])


def _ffn(h2, x1, w_up, conv_w, conv_b, w_down, g_final, seq):
    t = h2.shape[0]
    tm = FFN_TM
    hb = tm // HALO
    resident = lambda shape: pl.BlockSpec(shape, lambda i: (0, 0), pipeline_mode=pl.Buffered(1))
    return pl.pallas_call(
        functools.partial(_ffn_kernel, seq // tm),
        grid=(t // tm,),
        in_specs=[
            pl.BlockSpec((tm, D_MODEL), lambda i: (i, 0)),
            pl.BlockSpec((HALO, D_MODEL), lambda i: (jnp.maximum(i * hb - 1, 0), 0)),
            pl.BlockSpec((tm, D_MODEL), lambda i: (i, 0)),
            resident((D_MODEL, 2 * D_FF)),
            resident((TOPK, 2 * D_FF)),
            resident((1, 2 * D_FF)),
            resident((D_FF, D_MODEL)),
            resident((1, D_MODEL)),
        ],
        out_specs=pl.BlockSpec((tm, D_MODEL), lambda i: (i, 0)),
        out_shape=jax.ShapeDtypeStruct((t, D_MODEL), F32),
        scratch_shapes=[pltpu.VMEM((tm, D_FF), BF16)],
        compiler_params=pltpu.CompilerParams(dimension_semantics=("arbitrary",),
                                             vmem_limit_bytes=VMEM_LIMIT),
        name="ffn",
    )(h2, h2, x1, w_up, conv_w, conv_b, w_down, g_final)


def _rope_tables(seq):
    inv = ROPE_THETA ** (-np.arange(HALF, dtype=np.float64) / HALF)
    ang = np.arange(seq, dtype=np.float64)[:, None] * inv[None, :]
    cos, sin = np.cos(ang), np.sin(ang)
    ck = np.concatenate([cos, cos, cos, cos], axis=1)
    sk = np.concatenate([-sin, sin, -sin, sin], axis=1)
    q_scale = SCALE * LOG2E
    return tuple(jnp.asarray(t, F32) for t in (cos.T * q_scale, sin.T * q_scale, ck, sk))


def kernel(x, g_mix, w_in, b_gate, w_branch_a, w_branch_b, w_out, g_ffn, w_up, conv_w, conv_b,
           w_down, g_final):
    batch, seq, _ = x.shape
    assert seq % BLK == 0 and seq % FFN_TM == 0 and seq // BLK <= HALF
    assert g_mix.shape[0] == 1
    x2 = x.reshape(batch * seq, D_MODEL)
    cq, sq, ck, sk = _rope_tables(seq)
    for layer in range(1):
        w = w_in[layer]
        col = lambda k: w[:, k * WIDTH:(k + 1) * WIDTH]
        w_row = jnp.concatenate([col(1), col(4)], axis=1).astype(BF16)
        w_t = jnp.concatenate([col(0), col(2), col(3), col(5)], axis=1).T.astype(BF16)
        w_gate = w[:, 6 * WIDTH:].astype(BF16)
        gm = g_mix[layer][None, :]
        qa, ka, va, qb, kb, vb = _proj(x2, gm, w_row, w_t, cq, sq, ck, sk, batch, seq)
        ya_t = _moba(qa, ka, va, batch, seq)
        yb_t = _sb(qb, kb, vb, batch, seq)
        x1, h2 = _merge(x2, ya_t, yb_t, gm, w_gate, b_gate[layer][None, :],
                        w_branch_a[layer].astype(BF16), w_branch_b[layer].astype(BF16),
                        w_out[layer].astype(BF16), g_ffn[layer][None, :])
        x2 = _ffn(h2, x1, w_up[layer].astype(BF16), conv_w[layer], conv_b[layer][None, :],
                  w_down[layer].astype(BF16), g_final[None, :], seq)
    return x2.reshape(batch, seq, D_MODEL)
```

```python
import functools

import numpy as np
import jax
import jax.numpy as jnp
from jax import lax
from jax.experimental import pallas as pl
from jax.experimental.pallas import tpu as pltpu

F32 = jnp.float32
BF16 = jnp.bfloat16

D_MODEL = 1024
HEAD_DIM = 64
HALF = HEAD_DIM // 2
N_HEADS = 8
WIDTH = N_HEADS * HEAD_DIM
BLK = 256
TOPK = 3
ROPE_THETA = 10000.0
D_FF = 2816
RMS_EPS = 1e-6
NEG = -1e30
SCALE = HEAD_DIM ** -0.5

QA_ROWS = 128
VA_ROWS = 80
MOBA_CH = 2
MOBA_TQ = MOBA_CH * BLK
SB_STOP = 115.0
SB_HEADS = 4
LOG2E = float(np.log2(np.e))

MERGE_TM = 512
FFN_TM = 512
FFN_SUB = 256
HALO = 16

VMEM_LIMIT = 48 * 1024 * 1024


def _rms(x, g):
    ms = jnp.mean(x * x, axis=-1, keepdims=True)
    return x * lax.rsqrt(ms + RMS_EPS) * g


def _split_bf16(x):
    hi = x.astype(BF16)
    lo = (x - hi.astype(F32)).astype(BF16)
    return hi, lo


def _dot(a, b):
    return jnp.dot(a, b, preferred_element_type=F32)


def _dot_t(a_t, b):
    return lax.dot_general(a_t, b, (((0,), (0,)), ((), ())), preferred_element_type=F32)


def _proj_kernel(nb, x_ref, g_ref, wrow_ref, wt_ref, cq_ref, sq_ref, ck_ref, sk_ref,
                 qa_ref, ka_ref, va_ref, qb_ref, kb_ref, vb_ref, km_ref):
    blk = pl.program_id(0) % nb

    @pl.when(pl.program_id(0) == 0)
    def _():
        km_ref[...] = jnp.zeros_like(km_ref)

    h = _rms(x_ref[...], g_ref[...]).astype(BF16)
    rows = _dot(h, wrow_ref[...])
    cols = lax.dot_general(wt_ref[...], h, (((1,), (1,)), ((), ())),
                           preferred_element_type=F32)

    ka = rows[:, :WIDTH]
    lane = lax.broadcasted_iota(jnp.int32, (BLK, WIDTH), 1)
    first_half = (lane % HEAD_DIM) < HALF
    swapped = jnp.where(first_half, pltpu.roll(ka, WIDTH - HALF, 1), pltpu.roll(ka, HALF, 1))
    ck = jnp.concatenate([ck_ref[...]] * (WIDTH // 128), axis=1)
    sk = jnp.concatenate([sk_ref[...]] * (WIDTH // 128), axis=1)
    ka = ka * ck + swapped * sk
    km_ref[pl.ds(blk, 1), :] = jnp.mean(ka, axis=0, keepdims=True)
    onehot = (lax.broadcasted_iota(jnp.int32, (BLK, HEAD_DIM), 1) == blk).astype(F32)
    ka_blocks = []
    for hd in range(N_HEADS):
        ka_blocks += [ka[:, hd * HEAD_DIM:(hd + 1) * HEAD_DIM], onehot]
    ka_ref[0, 0] = jnp.concatenate(ka_blocks, axis=1).astype(BF16)
    kb_ref[0, 0] = rows[:, WIDTH:].astype(BF16)

    cq = cq_ref[...]
    sq = sq_ref[...]
    km = km_ref[...]
    n_iota = lax.broadcasted_iota(jnp.int32, (nb, BLK), 0).astype(F32)
    blk_f = blk.astype(F32)
    past = n_iota < blk_f
    zeros_q = jnp.zeros((QA_ROWS - HEAD_DIM - nb, BLK), F32)
    zeros_h = jnp.zeros((HEAD_DIM, BLK), F32)
    ones_v = jnp.ones((VA_ROWS - HEAD_DIM, BLK), F32)
    for hd in range(N_HEADS):
        r0 = hd * HEAD_DIM
        t1 = cols[r0:r0 + HALF]
        t2 = cols[r0 + HALF:r0 + HEAD_DIM]
        q = jnp.concatenate([t1 * cq - t2 * sq, t2 * cq + t1 * sq], axis=0)

        q_hi, q_lo = _split_bf16(q)
        km_hi, km_lo = _split_bf16(km[:, r0:r0 + HEAD_DIM])
        gate = _dot(km_hi, q_hi) + (_dot(km_hi, q_lo) + _dot(km_lo, q_hi))
        gate = jnp.where(past, gate, NEG)
        bias = jnp.where(n_iota == blk_f, 0.0, NEG)
        for r in range(TOPK):
            m = jnp.max(gate, axis=0, keepdims=True)
            idx = jnp.min(jnp.where(gate == m, n_iota, float(nb)), axis=0, keepdims=True)
            pick = n_iota == idx
            bias = jnp.where(pick, jnp.where(blk > r, 0.0, bias), bias)
            gate = jnp.where(pick, -jnp.inf, gate)
        qa_ref[0, hd] = jnp.concatenate([q, bias, zeros_q], axis=0).astype(BF16)

        va_ref[0, hd, 0] = jnp.concatenate([cols[WIDTH + r0:WIDTH + r0 + HEAD_DIM], ones_v],
                                           axis=0).astype(BF16)
        qb = cols[2 * WIDTH + r0:2 * WIDTH + r0 + HEAD_DIM] * (SCALE * LOG2E)
        qb_ref[0, hd] = jnp.concatenate([qb, zeros_h] if hd % 2 == 0 else [zeros_h, qb],
                                        axis=0).astype(BF16)
        vb_ref[0, hd, 0] = cols[3 * WIDTH + r0:3 * WIDTH + r0 + HEAD_DIM].astype(BF16)


def _proj(x2, g_mix, w_row, w_t, cq, sq, ck, sk, batch, seq):
    nb = seq // BLK
    grid = (batch * nb,)
    bi = lambda i: i // nb
    ji = lambda i: i % nb
    out_shape = (
        jax.ShapeDtypeStruct((batch, N_HEADS, QA_ROWS, seq), BF16),
        jax.ShapeDtypeStruct((batch, nb, BLK, N_HEADS * 128), BF16),
        jax.ShapeDtypeStruct((batch, N_HEADS, nb // MOBA_CH, VA_ROWS, MOBA_CH * BLK), BF16),
        jax.ShapeDtypeStruct((batch, N_HEADS, 2 * HEAD_DIM, seq), BF16),
        jax.ShapeDtypeStruct((batch, nb, BLK, WIDTH), BF16),
        jax.ShapeDtypeStruct((batch, N_HEADS, nb, HEAD_DIM, BLK), BF16),
    )
    in_specs = [
        pl.BlockSpec((BLK, D_MODEL), lambda i: (i, 0)),
        pl.BlockSpec((1, D_MODEL), lambda i: (0, 0)),
        pl.BlockSpec((D_MODEL, 2 * WIDTH), lambda i: (0, 0)),
        pl.BlockSpec((4 * WIDTH, D_MODEL), lambda i: (0, 0)),
        pl.BlockSpec((HALF, BLK), lambda i: (0, ji(i))),
        pl.BlockSpec((HALF, BLK), lambda i: (0, ji(i))),
        pl.BlockSpec((BLK, 128), lambda i: (ji(i), 0)),
        pl.BlockSpec((BLK, 128), lambda i: (ji(i), 0)),
    ]
    out_specs = (
        pl.BlockSpec((1, N_HEADS, QA_ROWS, BLK), lambda i: (bi(i), 0, 0, ji(i))),
        pl.BlockSpec((1, 1, BLK, N_HEADS * 128), lambda i: (bi(i), ji(i), 0, 0)),
        pl.BlockSpec((1, N_HEADS, 1, VA_ROWS, BLK),
                     lambda i: (bi(i), 0, ji(i) // MOBA_CH, 0, ji(i) % MOBA_CH)),
        pl.BlockSpec((1, N_HEADS, 2 * HEAD_DIM, BLK), lambda i: (bi(i), 0, 0, ji(i))),
        pl.BlockSpec((1, 1, BLK, WIDTH), lambda i: (bi(i), ji(i), 0, 0)),
        pl.BlockSpec((1, N_HEADS, 1, HEAD_DIM, BLK), lambda i: (bi(i), 0, ji(i), 0, 0)),
    )
    return pl.pallas_call(
        functools.partial(_proj_kernel, nb),
        grid=grid, in_specs=in_specs, out_specs=out_specs, out_shape=out_shape,
        scratch_shapes=[pltpu.VMEM((nb, WIDTH), F32)],
        compiler_params=pltpu.CompilerParams(dimension_semantics=("arbitrary",),
                                             vmem_limit_bytes=VMEM_LIMIT),
        name="proj",
    )(x2, g_mix, w_row, w_t, cq, sq, ck, sk)


def _moba_kernel(q_ref, qn_ref, k_ref, v_ref, o_ref, s_ref, m_ref, acc_ref, cmax_ref):
    n_past = pl.program_id(2)
    rows = MOBA_TQ
    q_pos = lax.broadcasted_iota(jnp.int32, (rows, rows), 1)
    k_pos = lax.broadcasted_iota(jnp.int32, (rows, rows), 0)

    def scores(c, slot, q=q_ref):
        for hh in range(2):
            cmax = None
            for u in range(MOBA_CH):
                s = _dot(k_ref[0, c * MOBA_CH + u, :, hh * 128:(hh + 1) * 128], q[0, hh])
                s_ref[slot, hh, u * BLK:(u + 1) * BLK, :] = s
                smax = jnp.max(s, axis=0, keepdims=True)
                cmax = smax if cmax is None else jnp.maximum(cmax, smax)
            cmax_ref[slot, hh] = cmax

    def attend(c, slot, own):
        for hh in range(2):
            s = s_ref[slot, hh]
            if own:
                s = jnp.where(k_pos <= q_pos, s, NEG)
                smax = jnp.max(s, axis=0, keepdims=True)
            else:
                smax = cmax_ref[slot, hh]
            m = m_ref[hh]
            m_new = jnp.maximum(m, smax)
            p = jnp.exp2(s - m_new).astype(BF16)
            acc_ref[hh] = acc_ref[hh] * jnp.exp2(m - m_new) + _dot(v_ref[0, hh, c], p)
            m_ref[hh] = m_new

    m_ref[...] = jnp.full_like(m_ref, NEG)
    acc_ref[...] = jnp.zeros_like(acc_ref)
    @pl.when(n_past == 0)
    def _():
        scores(0, 0)

    def next_tile_scores():
        scores(0, 0, qn_ref)

    def body(j, carry):
        c = 3 * j
        scores(c + 1, 1)
        attend(c, 0, False)
        scores(c + 2, 2)
        attend(c + 1, 1, False)
        scores(c + 3, 0)
        attend(c + 2, 2, False)
        return carry

    n_loop = n_past // 3
    lax.fori_loop(0, n_loop, body, 0)
    first = 3 * n_loop
    left = n_past - first

    @pl.when(left == 0)
    def _():
        attend(n_past, 0, True)
        next_tile_scores()

    @pl.when(left == 1)
    def _():
        scores(n_past, 1)
        attend(first, 0, False)
        next_tile_scores()
        attend(n_past, 1, True)

    @pl.when(left == 2)
    def _():
        scores(first + 1, 1)
        attend(first, 0, False)
        scores(n_past, 2)
        attend(first + 1, 1, False)
        next_tile_scores()
        attend(n_past, 2, True)

    outs = []
    for hh in range(2):
        acc = acc_ref[hh]
        outs.append(acc[:HEAD_DIM] / acc[HEAD_DIM:HEAD_DIM + 1])
    o_ref[0] = jnp.concatenate(outs, axis=0).astype(BF16)


def _moba(qa, ka, va, batch, seq):
    nb = seq // BLK
    nc = seq // MOBA_TQ
    return pl.pallas_call(
        _moba_kernel,
        grid=(batch, N_HEADS // 2, nc),
        in_specs=[
            pl.BlockSpec((1, 2, QA_ROWS, MOBA_TQ), lambda b, p, i: (b, p, 0, i)),
            pl.BlockSpec((1, 2, QA_ROWS, MOBA_TQ), lambda b, p, i: (b, p, 0, jnp.minimum(i + 1, nc - 1))),
            pl.BlockSpec((1, nb, BLK, 256), lambda b, p, i: (b, 0, 0, p)),
            pl.BlockSpec((1, 2, nc, VA_ROWS, MOBA_TQ), lambda b, p, i: (b, p, 0, 0, 0)),
        ],
        out_specs=pl.BlockSpec((1, 2 * HEAD_DIM, MOBA_TQ), lambda b, p, i: (b, p, i)),
        out_shape=jax.ShapeDtypeStruct((batch, WIDTH, seq), BF16),
        scratch_shapes=[pltpu.VMEM((3, 2, MOBA_TQ, MOBA_TQ), F32),
                        pltpu.VMEM((2, 1, MOBA_TQ), F32),
                        pltpu.VMEM((2, VA_ROWS, MOBA_TQ), F32),
                        pltpu.VMEM((3, 2, 1, MOBA_TQ), F32)],
        compiler_params=pltpu.CompilerParams(
            dimension_semantics=("arbitrary", "arbitrary", "arbitrary"),
            vmem_limit_bytes=VMEM_LIMIT),
        name="moba",
    )(qa, qa, ka, va)


def _sb_kernel(q_ref, k_ref, v_ref, o_ref, run_ref, acc_ref):
    i = pl.program_id(2)
    k_iota = lax.broadcasted_iota(jnp.int32, (BLK, BLK), 0)
    q_iota = lax.broadcasted_iota(jnp.int32, (BLK, BLK), 1)
    strict = k_iota < q_iota
    later = (q_iota > k_iota).astype(BF16)
    later2 = jnp.concatenate([later, later], axis=1)

    def pair_step(j, diag):
        has_old = j >= 1
        tiles = (j, jnp.maximum(j - 1, 0))
        heads = range(SB_HEADS)
        z, sp, w = {}, {}, {}
        for hh in heads:
            lanes = slice((hh // 2) * 128, (hh // 2 + 1) * 128)
            for u in range(2):
                z[hh, u] = _dot(k_ref[0, tiles[u], :, lanes], q_ref[0, hh])
        for hh in heads:
            for u in range(2):
                zz = z[hh, u]
                s = jnp.maximum(zz, 0.0) + jnp.log2(1.0 + jnp.exp2(-jnp.abs(zz)))
                if diag and u == 0:
                    s = jnp.where(strict, s, 0.0)
                sp[hh, u] = s
                w[hh, u] = jnp.concatenate(_split_bf16(s), axis=0)
        aft = {key: _dot(later2, w[key]) for key in w}
        for hh in heads:
            run = run_ref[hh]
            pv = None
            for u in range(2):
                if u == 1:
                    run = jnp.where(has_old, run, NEG)
                a = jnp.exp2((z[hh, u] - sp[hh, u]) - aft[hh, u] + run)
                if diag and u == 0:
                    a = jnp.where(strict, a, 0.0)
                part = _dot(v_ref[0, hh, tiles[u]], a.astype(BF16))
                pv = part if pv is None else pv + part
                run = run - (aft[hh, u][0:1] + sp[hh, u][0:1])
            acc_ref[hh] += pv
            run_ref[hh] = run

    run_ref[...] = jnp.zeros_like(run_ref)
    acc_ref[...] = jnp.zeros_like(acc_ref)
    pair_step(i, True)

    def cond(j):
        return jnp.logical_and(j >= 0, jnp.max(run_ref[...]) > -SB_STOP * LOG2E)

    def body(j):
        pair_step(j, False)
        return j - 2

    lax.while_loop(cond, body, i - 2)
    o_ref[0] = jnp.concatenate([acc_ref[hh] for hh in range(SB_HEADS)], axis=0).astype(BF16)


def _sb(qb, kb, vb, batch, seq):
    nb = seq // BLK
    return pl.pallas_call(
        _sb_kernel,
        grid=(batch, N_HEADS // SB_HEADS, nb),
        in_specs=[
            pl.BlockSpec((1, SB_HEADS, 2 * HEAD_DIM, BLK), lambda b, g, i: (b, g, 0, i)),
            pl.BlockSpec((1, nb, BLK, SB_HEADS * HEAD_DIM), lambda b, g, i: (b, 0, 0, g)),
            pl.BlockSpec((1, SB_HEADS, nb, HEAD_DIM, BLK), lambda b, g, i: (b, g, 0, 0, 0)),
        ],
        out_specs=pl.BlockSpec((1, SB_HEADS * HEAD_DIM, BLK), lambda b, g, i: (b, g, i)),
        out_shape=jax.ShapeDtypeStruct((batch, WIDTH, seq), BF16),
        scratch_shapes=[pltpu.VMEM((SB_HEADS, 1, BLK), F32),
                        pltpu.VMEM((SB_HEADS, HEAD_DIM, BLK), F32)],
        compiler_params=pltpu.CompilerParams(
            dimension_semantics=("arbitrary", "arbitrary", "arbitrary"),
            vmem_limit_bytes=VMEM_LIMIT),
        name="sb",
    )(qb, kb, vb)


def _merge_kernel(x_ref, ya_ref, yb_ref, gm_ref, wg_ref, bg_ref, wa_ref, wb_ref, wo_ref, gf_ref,
                  x1_ref, h2_ref):
    x = x_ref[...]
    h = _rms(x, gm_ref[...]).astype(BF16)
    gates = jax.nn.sigmoid(_dot(h, wg_ref[...]) + bg_ref[...])
    merged = (gates[:, :D_MODEL] * _dot_t(ya_ref[0], wa_ref[...])
              + gates[:, D_MODEL:] * _dot_t(yb_ref[0], wb_ref[...]))
    x1 = x + _dot(merged.astype(BF16), wo_ref[...])
    x1_ref[...] = x1
    h2_ref[...] = _rms(x1, gf_ref[...]).astype(BF16)


def _merge(x2, ya_t, yb_t, g_mix, w_gate, b_gate, w_a, w_b, w_out, g_ffn):
    t = x2.shape[0]
    tm = MERGE_TM
    tiles_per_seq = ya_t.shape[2] // tm
    full = lambda shape: pl.BlockSpec(shape, lambda i: (0, 0))
    y_spec = pl.BlockSpec((1, WIDTH, tm), lambda i: (i // tiles_per_seq, 0, i % tiles_per_seq))
    return pl.pallas_call(
        _merge_kernel,
        grid=(t // tm,),
        in_specs=[
            pl.BlockSpec((tm, D_MODEL), lambda i: (i, 0)),
            y_spec,
            y_spec,
            full((1, D_MODEL)),
            full((D_MODEL, 2 * D_MODEL)),
            full((1, 2 * D_MODEL)),
            full((WIDTH, D_MODEL)),
            full((WIDTH, D_MODEL)),
            full((D_MODEL, D_MODEL)),
            full((1, D_MODEL)),
        ],
        out_specs=(pl.BlockSpec((tm, D_MODEL), lambda i: (i, 0)),
                   pl.BlockSpec((tm, D_MODEL), lambda i: (i, 0))),
        out_shape=(jax.ShapeDtypeStruct((t, D_MODEL), F32),
                   jax.ShapeDtypeStruct((t, D_MODEL), BF16)),
        compiler_params=pltpu.CompilerParams(dimension_semantics=("arbitrary",),
                                             vmem_limit_bytes=VMEM_LIMIT),
        name="merge",
    )(x2, ya_t, yb_t, g_mix, w_gate, b_gate, w_a, w_b, w_out, g_ffn)


def _ffn_kernel(tiles_per_seq, h_ref, halo_ref, x1_ref, wu_ref, cw_ref, cb_ref, wd_ref, gfin_ref,
                o_ref, act_ref):
    i = pl.program_id(0)
    halo = jnp.where(i % tiles_per_seq == 0, jnp.zeros_like(halo_ref[...]), halo_ref[...])
    hx = jnp.concatenate([halo, h_ref[...]], axis=0)

    for s in range(D_FF // FFN_SUB):
        conv = []
        for half in range(2):
            col = half * D_FF + s * FFN_SUB
            u = _dot(hx, wu_ref[:, col:col + FFN_SUB])
            cw = cw_ref[:, col:col + FFN_SUB]
            taps = pltpu.roll(u, 2, 0) * cw[0:1] + pltpu.roll(u, 1, 0) * cw[1:2] + u * cw[2:3]
            conv.append(taps[HALO:] + cb_ref[:, col:col + FFN_SUB])
        ug, uv = conv
        act_ref[:, s * FFN_SUB:(s + 1) * FFN_SUB] = (ug * jax.nn.sigmoid(ug) * uv).astype(BF16)
    y = x1_ref[...] + _dot(act_ref[...], wd_ref[...])
    o_ref[...] = _rms(y, gfin_ref[...])


def _ffn(h2, x1, w_up, conv_w, conv_b, w_down, g_final, seq):
    t = h2.shape[0]
    tm = FFN_TM
    hb = tm // HALO
    resident = lambda shape: pl.BlockSpec(shape, lambda i: (0, 0), pipeline_mode=pl.Buffered(1))
    return pl.pallas_call(
        functools.partial(_ffn_kernel, seq // tm),
        grid=(t // tm,),
        in_specs=[
            pl.BlockSpec((tm, D_MODEL), lambda i: (i, 0)),
            pl.BlockSpec((HALO, D_MODEL), lambda i: (jnp.maximum(i * hb - 1, 0), 0)),
            pl.BlockSpec((tm, D_MODEL), lambda i: (i, 0)),
            resident((D_MODEL, 2 * D_FF)),
            resident((TOPK, 2 * D_FF)),
            resident((1, 2 * D_FF)),
            resident((D_FF, D_MODEL)),
            resident((1, D_MODEL)),
        ],
        out_specs=pl.BlockSpec((tm, D_MODEL), lambda i: (i, 0)),
        out_shape=jax.ShapeDtypeStruct((t, D_MODEL), F32),
        scratch_shapes=[pltpu.VMEM((tm, D_FF), BF16)],
        compiler_params=pltpu.CompilerParams(dimension_semantics=("arbitrary",),
                                             vmem_limit_bytes=VMEM_LIMIT),
        name="ffn",
    )(h2, h2, x1, w_up, conv_w, conv_b, w_down, g_final)


def _rope_tables(seq):
    inv = ROPE_THETA ** (-np.arange(HALF, dtype=np.float64) / HALF)
    ang = np.arange(seq, dtype=np.float64)[:, None] * inv[None, :]
    cos, sin = np.cos(ang), np.sin(ang)
    ck = np.concatenate([cos, cos, cos, cos], axis=1)
    sk = np.concatenate([-sin, sin, -sin, sin], axis=1)
    q_scale = SCALE * LOG2E
    return tuple(jnp.asarray(t, F32) for t in (cos.T * q_scale, sin.T * q_scale, ck, sk))


def kernel(x, g_mix, w_in, b_gate, w_branch_a, w_branch_b, w_out, g_ffn, w_up, conv_w, conv_b,
           w_down, g_final):
    batch, seq, _ = x.shape
    assert seq % BLK == 0 and seq % FFN_TM == 0 and seq // BLK <= HALF
    assert g_mix.shape[0] == 1
    x2 = x.reshape(batch * seq, D_MODEL)
    cq, sq, ck, sk = _rope_tables(seq)
    for layer in range(1):
        w = w_in[layer]
        col = lambda k: w[:, k * WIDTH:(k + 1) * WIDTH]
        w_row = jnp.concatenate([col(1), col(4)], axis=1).astype(BF16)
        w_t = jnp.concatenate([col(0), col(2), col(3), col(5)], axis=1).T.astype(BF16)
        w_gate = w[:, 6 * WIDTH:].astype(BF16)
        gm = g_mix[layer][None, :]
        qa, ka, va, qb, kb, vb = _proj(x2, gm, w_row, w_t, cq, sq, ck, sk, batch, seq)
        ya_t = _moba(qa, ka, va, batch, seq)
        yb_t = _sb(qb, kb, vb, batch, seq)
        x1, h2 = _merge(x2, ya_t, yb_t, gm, w_gate, b_gate[layer][None, :],
                        w_branch_a[layer].astype(BF16), w_branch_b[layer].astype(BF16),
                        w_out[layer].astype(BF16), g_ffn[layer][None, :])
        x2 = _ffn(h2, x1, w_up[layer].astype(BF16), conv_w[layer], conv_b[layer][None, :],
                  w_down[layer].astype(BF16), g_final[None, :], seq)
    return x2.reshape(batch, seq, D_MODEL)
```

```python
import functools

import numpy as np
import jax
import jax.numpy as jnp
from jax import lax
from jax.experimental import pallas as pl
from jax.experimental.pallas import tpu as pltpu

F32 = jnp.float32
BF16 = jnp.bfloat16

D_MODEL = 1024
HEAD_DIM = 64
HALF = HEAD_DIM // 2
N_HEADS = 8
WIDTH = N_HEADS * HEAD_DIM
BLK = 256
TOPK = 3
ROPE_THETA = 10000.0
D_FF = 2816
RMS_EPS = 1e-6
NEG = -1e30
SCALE = HEAD_DIM ** -0.5

QA_ROWS = 128
VA_ROWS = 80
MOBA_CH = 2
MOBA_TQ = MOBA_CH * BLK
SB_STOP = 115.0
SB_HEADS = 4
SB_QT = 2
LOG2E = float(np.log2(np.e))

MERGE_TM = 512
FFN_TM = 512
FFN_SUB = 256
HALO = 16

VMEM_LIMIT = 48 * 1024 * 1024


def _rms(x, g):
    ms = jnp.mean(x * x, axis=-1, keepdims=True)
    return x * lax.rsqrt(ms + RMS_EPS) * g


def _split_bf16(x):
    hi = x.astype(BF16)
    lo = (x - hi.astype(F32)).astype(BF16)
    return hi, lo


def _dot(a, b):
    return jnp.dot(a, b, preferred_element_type=F32)


def _dot_t(a_t, b):
    return lax.dot_general(a_t, b, (((0,), (0,)), ((), ())), preferred_element_type=F32)


def _proj_kernel(nb, x_ref, g_ref, wrow_ref, wt_ref, cq_ref, sq_ref, ck_ref, sk_ref,
                 qa_ref, ka_ref, va_ref, qb_ref, kb_ref, vb_ref, km_ref):
    blk = pl.program_id(0) % nb

    @pl.when(pl.program_id(0) == 0)
    def _():
        km_ref[...] = jnp.zeros_like(km_ref)

    h = _rms(x_ref[...], g_ref[...]).astype(BF16)
    rows = _dot(h, wrow_ref[...])
    cols = lax.dot_general(wt_ref[...], h, (((1,), (1,)), ((), ())),
                           preferred_element_type=F32)

    ka = rows[:, :WIDTH]
    lane = lax.broadcasted_iota(jnp.int32, (BLK, WIDTH), 1)
    first_half = (lane % HEAD_DIM) < HALF
    swapped = jnp.where(first_half, pltpu.roll(ka, WIDTH - HALF, 1), pltpu.roll(ka, HALF, 1))
    ck = jnp.concatenate([ck_ref[...]] * (WIDTH // 128), axis=1)
    sk = jnp.concatenate([sk_ref[...]] * (WIDTH // 128), axis=1)
    ka = ka * ck + swapped * sk
    km_ref[pl.ds(blk, 1), :] = jnp.mean(ka, axis=0, keepdims=True)
    onehot = (lax.broadcasted_iota(jnp.int32, (BLK, HEAD_DIM), 1) == blk).astype(F32)
    ka_blocks = []
    for hd in range(N_HEADS):
        ka_blocks += [ka[:, hd * HEAD_DIM:(hd + 1) * HEAD_DIM], onehot]
    ka_ref[0, 0] = jnp.concatenate(ka_blocks, axis=1).astype(BF16)
    kb_ref[0, 0] = rows[:, WIDTH:].astype(BF16)

    cq = cq_ref[...]
    sq = sq_ref[...]
    km = km_ref[...]
    n_iota = lax.broadcasted_iota(jnp.int32, (nb, BLK), 0).astype(F32)
    blk_f = blk.astype(F32)
    past = n_iota < blk_f
    zeros_q = jnp.zeros((QA_ROWS - HEAD_DIM - nb, BLK), F32)
    zeros_h = jnp.zeros((HEAD_DIM, BLK), F32)
    ones_v = jnp.ones((VA_ROWS - HEAD_DIM, BLK), F32)
    for hd in range(N_HEADS):
        r0 = hd * HEAD_DIM
        t1 = cols[r0:r0 + HALF]
        t2 = cols[r0 + HALF:r0 + HEAD_DIM]
        q = jnp.concatenate([t1 * cq - t2 * sq, t2 * cq + t1 * sq], axis=0)

        q_hi, q_lo = _split_bf16(q)
        km_hi, km_lo = _split_bf16(km[:, r0:r0 + HEAD_DIM])
        gate = _dot(km_hi, q_hi) + (_dot(km_hi, q_lo) + _dot(km_lo, q_hi))
        gate = jnp.where(past, gate, NEG)
        bias = jnp.where(n_iota == blk_f, 0.0, NEG)
        for r in range(TOPK):
            m = jnp.max(gate, axis=0, keepdims=True)
            idx = jnp.min(jnp.where(gate == m, n_iota, float(nb)), axis=0, keepdims=True)
            pick = n_iota == idx
            bias = jnp.where(pick, jnp.where(blk > r, 0.0, bias), bias)
            gate = jnp.where(pick, -jnp.inf, gate)
        qa_ref[0, hd] = jnp.concatenate([q, bias, zeros_q], axis=0).astype(BF16)

        va_ref[0, hd, 0] = jnp.concatenate([cols[WIDTH + r0:WIDTH + r0 + HEAD_DIM], ones_v],
                                           axis=0).astype(BF16)
        qb = cols[2 * WIDTH + r0:2 * WIDTH + r0 + HEAD_DIM] * (SCALE * LOG2E)
        qb_ref[0, hd] = jnp.concatenate([qb, zeros_h] if hd % 2 == 0 else [zeros_h, qb],
                                        axis=0).astype(BF16)
        vb_ref[0, hd, 0] = cols[3 * WIDTH + r0:3 * WIDTH + r0 + HEAD_DIM].astype(BF16)


def _proj(x2, g_mix, w_row, w_t, cq, sq, ck, sk, batch, seq):
    nb = seq // BLK
    grid = (batch * nb,)
    bi = lambda i: i // nb
    ji = lambda i: i % nb
    out_shape = (
        jax.ShapeDtypeStruct((batch, N_HEADS, QA_ROWS, seq), BF16),
        jax.ShapeDtypeStruct((batch, nb, BLK, N_HEADS * 128), BF16),
        jax.ShapeDtypeStruct((batch, N_HEADS, nb // MOBA_CH, VA_ROWS, MOBA_CH * BLK), BF16),
        jax.ShapeDtypeStruct((batch, N_HEADS, 2 * HEAD_DIM, seq), BF16),
        jax.ShapeDtypeStruct((batch, nb, BLK, WIDTH), BF16),
        jax.ShapeDtypeStruct((batch, N_HEADS, nb, HEAD_DIM, BLK), BF16),
    )
    in_specs = [
        pl.BlockSpec((BLK, D_MODEL), lambda i: (i, 0)),
        pl.BlockSpec((1, D_MODEL), lambda i: (0, 0)),
        pl.BlockSpec((D_MODEL, 2 * WIDTH), lambda i: (0, 0)),
        pl.BlockSpec((4 * WIDTH, D_MODEL), lambda i: (0, 0)),
        pl.BlockSpec((HALF, BLK), lambda i: (0, ji(i))),
        pl.BlockSpec((HALF, BLK), lambda i: (0, ji(i))),
        pl.BlockSpec((BLK, 128), lambda i: (ji(i), 0)),
        pl.BlockSpec((BLK, 128), lambda i: (ji(i), 0)),
    ]
    out_specs = (
        pl.BlockSpec((1, N_HEADS, QA_ROWS, BLK), lambda i: (bi(i), 0, 0, ji(i))),
        pl.BlockSpec((1, 1, BLK, N_HEADS * 128), lambda i: (bi(i), ji(i), 0, 0)),
        pl.BlockSpec((1, N_HEADS, 1, VA_ROWS, BLK),
                     lambda i: (bi(i), 0, ji(i) // MOBA_CH, 0, ji(i) % MOBA_CH)),
        pl.BlockSpec((1, N_HEADS, 2 * HEAD_DIM, BLK), lambda i: (bi(i), 0, 0, ji(i))),
        pl.BlockSpec((1, 1, BLK, WIDTH), lambda i: (bi(i), ji(i), 0, 0)),
        pl.BlockSpec((1, N_HEADS, 1, HEAD_DIM, BLK), lambda i: (bi(i), 0, ji(i), 0, 0)),
    )
    return pl.pallas_call(
        functools.partial(_proj_kernel, nb),
        grid=grid, in_specs=in_specs, out_specs=out_specs, out_shape=out_shape,
        scratch_shapes=[pltpu.VMEM((nb, WIDTH), F32)],
        compiler_params=pltpu.CompilerParams(dimension_semantics=("arbitrary",),
                                             vmem_limit_bytes=VMEM_LIMIT),
        name="proj",
    )(x2, g_mix, w_row, w_t, cq, sq, ck, sk)


def _moba_kernel(q_ref, qn_ref, k_ref, v_ref, o_ref, s_ref, m_ref, acc_ref, cmax_ref):
    n_past = pl.program_id(2)
    rows = MOBA_TQ
    q_pos = lax.broadcasted_iota(jnp.int32, (rows, rows), 1)
    k_pos = lax.broadcasted_iota(jnp.int32, (rows, rows), 0)

    def scores(c, slot, q=q_ref):
        for hh in range(2):
            cmax = None
            for u in range(MOBA_CH):
                s = _dot(k_ref[0, c * MOBA_CH + u, :, hh * 128:(hh + 1) * 128], q[0, hh])
                s_ref[slot, hh, u * BLK:(u + 1) * BLK, :] = s
                smax = jnp.max(s, axis=0, keepdims=True)
                cmax = smax if cmax is None else jnp.maximum(cmax, smax)
            cmax_ref[slot, hh] = cmax

    def attend(c, slot, own):
        for hh in range(2):
            s = s_ref[slot, hh]
            if own:
                s = jnp.where(k_pos <= q_pos, s, NEG)
                smax = jnp.max(s, axis=0, keepdims=True)
            else:
                smax = cmax_ref[slot, hh]
            m = m_ref[hh]
            m_new = jnp.maximum(m, smax)
            p = jnp.exp2(s - m_new).astype(BF16)
            acc_ref[hh] = acc_ref[hh] * jnp.exp2(m - m_new) + _dot(v_ref[0, hh, c], p)
            m_ref[hh] = m_new

    m_ref[...] = jnp.full_like(m_ref, NEG)
    acc_ref[...] = jnp.zeros_like(acc_ref)
    @pl.when(n_past == 0)
    def _():
        scores(0, 0)

    def next_tile_scores():
        scores(0, 0, qn_ref)

    def body(j, carry):
        c = 3 * j
        scores(c + 1, 1)
        attend(c, 0, False)
        scores(c + 2, 2)
        attend(c + 1, 1, False)
        scores(c + 3, 0)
        attend(c + 2, 2, False)
        return carry

    n_loop = n_past // 3
    lax.fori_loop(0, n_loop, body, 0)
    first = 3 * n_loop
    left = n_past - first

    @pl.when(left == 0)
    def _():
        attend(n_past, 0, True)
        next_tile_scores()

    @pl.when(left == 1)
    def _():
        scores(n_past, 1)
        attend(first, 0, False)
        next_tile_scores()
        attend(n_past, 1, True)

    @pl.when(left == 2)
    def _():
        scores(first + 1, 1)
        attend(first, 0, False)
        scores(n_past, 2)
        attend(first + 1, 1, False)
        next_tile_scores()
        attend(n_past, 2, True)

    outs = []
    for hh in range(2):
        acc = acc_ref[hh]
        outs.append(acc[:HEAD_DIM] / acc[HEAD_DIM:HEAD_DIM + 1])
    o_ref[0] = jnp.concatenate(outs, axis=0).astype(BF16)


def _moba(qa, ka, va, batch, seq):
    nb = seq // BLK
    nc = seq // MOBA_TQ
    return pl.pallas_call(
        _moba_kernel,
        grid=(batch, N_HEADS // 2, nc),
        in_specs=[
            pl.BlockSpec((1, 2, QA_ROWS, MOBA_TQ), lambda b, p, i: (b, p, 0, i)),
            pl.BlockSpec((1, 2, QA_ROWS, MOBA_TQ), lambda b, p, i: (b, p, 0, jnp.minimum(i + 1, nc - 1))),
            pl.BlockSpec((1, nb, BLK, 256), lambda b, p, i: (b, 0, 0, p)),
            pl.BlockSpec((1, 2, nc, VA_ROWS, MOBA_TQ), lambda b, p, i: (b, p, 0, 0, 0)),
        ],
        out_specs=pl.BlockSpec((1, 2 * HEAD_DIM, MOBA_TQ), lambda b, p, i: (b, p, i)),
        out_shape=jax.ShapeDtypeStruct((batch, WIDTH, seq), BF16),
        scratch_shapes=[pltpu.VMEM((3, 2, MOBA_TQ, MOBA_TQ), F32),
                        pltpu.VMEM((2, 1, MOBA_TQ), F32),
                        pltpu.VMEM((2, VA_ROWS, MOBA_TQ), F32),
                        pltpu.VMEM((3, 2, 1, MOBA_TQ), F32)],
        compiler_params=pltpu.CompilerParams(
            dimension_semantics=("arbitrary", "arbitrary", "arbitrary"),
            vmem_limit_bytes=VMEM_LIMIT),
        name="moba",
    )(qa, qa, ka, va)


def _sb_kernel(qall_ref, k_ref, v_ref, o_ref, run_all, acc_all):
    i0 = pl.program_id(2) * SB_QT
    k_iota = lax.broadcasted_iota(jnp.int32, (BLK, BLK), 0)
    q_iota = lax.broadcasted_iota(jnp.int32, (BLK, BLK), 1)
    strict = k_iota < q_iota
    later = (q_iota > k_iota).astype(BF16)
    later2 = jnp.concatenate([later, later], axis=1)

    def pair_step(j, diag, q_ref, run_ref, acc_ref):
        has_old = j >= 1
        tiles = (j, jnp.maximum(j - 1, 0))
        heads = range(SB_HEADS)
        z, sp, w = {}, {}, {}
        for hh in heads:
            lanes = slice((hh // 2) * 128, (hh // 2 + 1) * 128)
            for u in range(2):
                z[hh, u] = _dot(k_ref[0, tiles[u], :, lanes], q_ref[0, hh])
        for hh in heads:
            for u in range(2):
                zz = z[hh, u]
                s = jnp.maximum(zz, 0.0) + jnp.log2(1.0 + jnp.exp2(-jnp.abs(zz)))
                if diag and u == 0:
                    s = jnp.where(strict, s, 0.0)
                sp[hh, u] = s
                w[hh, u] = jnp.concatenate(_split_bf16(s), axis=0)
        aft = {key: _dot(later2, w[key]) for key in w}
        for hh in heads:
            run = run_ref[hh]
            pv = None
            for u in range(2):
                if u == 1:
                    run = jnp.where(has_old, run, NEG)
                a = jnp.exp2((z[hh, u] - sp[hh, u]) - aft[hh, u] + run)
                if diag and u == 0:
                    a = jnp.where(strict, a, 0.0)
                part = _dot(v_ref[0, hh, tiles[u]], a.astype(BF16))
                pv = part if pv is None else pv + part
                run = run - (aft[hh, u][0:1] + sp[hh, u][0:1])
            acc_ref[hh] += pv
            run_ref[hh] = run

    run_all[...] = jnp.zeros_like(run_all)
    acc_all[...] = jnp.zeros_like(acc_all)
    views = [(qall_ref.at[:, :, :, t * BLK:(t + 1) * BLK], run_all.at[t], acc_all.at[t])
             for t in range(SB_QT)]
    for t in range(SB_QT):
        pair_step(i0 + t, True, *views[t])
    for t in range(SB_QT):
        def cond(j, t=t):
            return jnp.logical_and(j >= 0, jnp.max(run_all[t]) > -SB_STOP * LOG2E)

        def body(j, t=t):
            pair_step(j, False, *views[t])
            return j - 2

        lax.while_loop(cond, body, i0 + t - 2)
    o_ref[0] = jnp.concatenate(
        [jnp.concatenate([acc_all[t, hh] for hh in range(SB_HEADS)], axis=0) for t in range(SB_QT)],
        axis=1).astype(BF16)


def _sb(qb, kb, vb, batch, seq):
    nb = seq // BLK
    return pl.pallas_call(
        _sb_kernel,
        grid=(batch, N_HEADS // SB_HEADS, nb // SB_QT),
        in_specs=[
            pl.BlockSpec((1, SB_HEADS, 2 * HEAD_DIM, SB_QT * BLK), lambda b, g, i: (b, g, 0, i)),
            pl.BlockSpec((1, nb, BLK, SB_HEADS * HEAD_DIM), lambda b, g, i: (b, 0, 0, g)),
            pl.BlockSpec((1, SB_HEADS, nb, HEAD_DIM, BLK), lambda b, g, i: (b, g, 0, 0, 0)),
        ],
        out_specs=pl.BlockSpec((1, SB_HEADS * HEAD_DIM, SB_QT * BLK), lambda b, g, i: (b, g, i)),
        out_shape=jax.ShapeDtypeStruct((batch, WIDTH, seq), BF16),
        scratch_shapes=[pltpu.VMEM((SB_QT, SB_HEADS, 1, BLK), F32),
                        pltpu.VMEM((SB_QT, SB_HEADS, HEAD_DIM, BLK), F32)],
        compiler_params=pltpu.CompilerParams(
            dimension_semantics=("arbitrary", "arbitrary", "arbitrary"),
            vmem_limit_bytes=VMEM_LIMIT),
        name="sb",
    )(qb, kb, vb)


def _merge_kernel(x_ref, ya_ref, yb_ref, gm_ref, wg_ref, bg_ref, wa_ref, wb_ref, wo_ref, gf_ref,
                  x1_ref, h2_ref):
    x = x_ref[...]
    h = _rms(x, gm_ref[...]).astype(BF16)
    gates = jax.nn.sigmoid(_dot(h, wg_ref[...]) + bg_ref[...])
    merged = (gates[:, :D_MODEL] * _dot_t(ya_ref[0], wa_ref[...])
              + gates[:, D_MODEL:] * _dot_t(yb_ref[0], wb_ref[...]))
    x1 = x + _dot(merged.astype(BF16), wo_ref[...])
    x1_ref[...] = x1
    h2_ref[...] = _rms(x1, gf_ref[...]).astype(BF16)


def _merge(x2, ya_t, yb_t, g_mix, w_gate, b_gate, w_a, w_b, w_out, g_ffn):
    t = x2.shape[0]
    tm = MERGE_TM
    tiles_per_seq = ya_t.shape[2] // tm
    full = lambda shape: pl.BlockSpec(shape, lambda i: (0, 0))
    y_spec = pl.BlockSpec((1, WIDTH, tm), lambda i: (i // tiles_per_seq, 0, i % tiles_per_seq))
    return pl.pallas_call(
        _merge_kernel,
        grid=(t // tm,),
        in_specs=[
            pl.BlockSpec((tm, D_MODEL), lambda i: (i, 0)),
            y_spec,
            y_spec,
            full((1, D_MODEL)),
            full((D_MODEL, 2 * D_MODEL)),
            full((1, 2 * D_MODEL)),
            full((WIDTH, D_MODEL)),
            full((WIDTH, D_MODEL)),
            full((D_MODEL, D_MODEL)),
            full((1, D_MODEL)),
        ],
        out_specs=(pl.BlockSpec((tm, D_MODEL), lambda i: (i, 0)),
                   pl.BlockSpec((tm, D_MODEL), lambda i: (i, 0))),
        out_shape=(jax.ShapeDtypeStruct((t, D_MODEL), F32),
                   jax.ShapeDtypeStruct((t, D_MODEL), BF16)),
        compiler_params=pltpu.CompilerParams(dimension_semantics=("arbitrary",),
                                             vmem_limit_bytes=VMEM_LIMIT),
        name="merge",
    )(x2, ya_t, yb_t, g_mix, w_gate, b_gate, w_a, w_b, w_out, g_ffn)


def _ffn_kernel(tiles_per_seq, h_ref, halo_ref, x1_ref, wu_ref, cw_ref, cb_ref, wd_ref, gfin_ref,
                o_ref, act_ref):
    i = pl.program_id(0)
    halo = jnp.where(i % tiles_per_seq == 0, jnp.zeros_like(halo_ref[...]), halo_ref[...])
    hx = jnp.concatenate([halo, h_ref[...]], axis=0)

    for s in range(D_FF // FFN_SUB):
        conv = []
        for half in range(2):
            col = half * D_FF + s * FFN_SUB
            u = _dot(hx, wu_ref[:, col:col + FFN_SUB])
            cw = cw_ref[:, col:col + FFN_SUB]
            taps = pltpu.roll(u, 2, 0) * cw[0:1] + pltpu.roll(u, 1, 0) * cw[1:2] + u * cw[2:3]
            conv.append(taps[HALO:] + cb_ref[:, col:col + FFN_SUB])
        ug, uv = conv
        act_ref[:, s * FFN_SUB:(s + 1) * FFN_SUB] = (ug * jax.nn.sigmoid(ug) * uv).astype(BF16)
    y = x1_ref[...] + _dot(act_ref[...], wd_ref[...])
    o_ref[...] = _rms(y, gfin_ref[...])


def _ffn(h2, x1, w_up, conv_w, conv_b, w_down, g_final, seq):
    t = h2.shape[0]
    tm = FFN_TM
    hb = tm // HALO
    resident = lambda shape: pl.BlockSpec(shape, lambda i: (0, 0), pipeline_mode=pl.Buffered(1))
    return pl.pallas_call(
        functools.partial(_ffn_kernel, seq // tm),
        grid=(t // tm,),
        in_specs=[
            pl.BlockSpec((tm, D_MODEL), lambda i: (i, 0)),
            pl.BlockSpec((HALO, D_MODEL), lambda i: (jnp.maximum(i * hb - 1, 0), 0)),
            pl.BlockSpec((tm, D_MODEL), lambda i: (i, 0)),
            resident((D_MODEL, 2 * D_FF)),
            resident((TOPK, 2 * D_FF)),
            resident((1, 2 * D_FF)),
            resident((D_FF, D_MODEL)),
            resident((1, D_MODEL)),
        ],
        out_specs=pl.BlockSpec((tm, D_MODEL), lambda i: (i, 0)),
        out_shape=jax.ShapeDtypeStruct((t, D_MODEL), F32),
        scratch_shapes=[pltpu.VMEM((tm, D_FF), BF16)],
        compiler_params=pltpu.CompilerParams(dimension_semantics=("arbitrary",),
                                             vmem_limit_bytes=VMEM_LIMIT),
        name="ffn",
    )(h2, h2, x1, w_up, conv_w, conv_b, w_down, g_final)


def _rope_tables(seq):
    inv = ROPE_THETA ** (-np.arange(HALF, dtype=np.float64) / HALF)
    ang = np.arange(seq, dtype=np.float64)[:, None] * inv[None, :]
    cos, sin = np.cos(ang), np.sin(ang)
    ck = np.concatenate([cos, cos, cos, cos], axis=1)
    sk = np.concatenate([-sin, sin, -sin, sin], axis=1)
    q_scale = SCALE * LOG2E
    return tuple(jnp.asarray(t, F32) for t in (cos.T * q_scale, sin.T * q_scale, ck, sk))


def kernel(x, g_mix, w_in, b_gate, w_branch_a, w_branch_b, w_out, g_ffn, w_up, conv_w, conv_b,
           w_down, g_final):
    batch, seq, _ = x.shape
    assert seq % BLK == 0 and seq % FFN_TM == 0 and seq // BLK <= HALF
    assert g_mix.shape[0] == 1
    x2 = x.reshape(batch * seq, D_MODEL)
    cq, sq, ck, sk = _rope_tables(seq)
    for layer in range(1):
        w = w_in[layer]
        col = lambda k: w[:, k * WIDTH:(k + 1) * WIDTH]
        w_row = jnp.concatenate([col(1), col(4)], axis=1).astype(BF16)
        w_t = jnp.concatenate([col(0), col(2), col(3), col(5)], axis=1).T.astype(BF16)
        w_gate = w[:, 6 * WIDTH:].astype(BF16)
        gm = g_mix[layer][None, :]
        qa, ka, va, qb, kb, vb = _proj(x2, gm, w_row, w_t, cq, sq, ck, sk, batch, seq)
        ya_t = _moba(qa, ka, va, batch, seq)
        yb_t = _sb(qb, kb, vb, batch, seq)
        x1, h2 = _merge(x2, ya_t, yb_t, gm, w_gate, b_gate[layer][None, :],
                        w_branch_a[layer].astype(BF16), w_branch_b[layer].astype(BF16),
                        w_out[layer].astype(BF16), g_ffn[layer][None, :])
        x2 = _ffn(h2, x1, w_up[layer].astype(BF16), conv_w[layer], conv_b[layer][None, :],
                  w_down[layer].astype(BF16), g_final[None, :], seq)
    return x2.reshape(batch, seq, D_MODEL)
```

```python
import functools

import numpy as np
import jax
import jax.numpy as jnp
from jax import lax
from jax.experimental import pallas as pl
from jax.experimental.pallas import tpu as pltpu

F32 = jnp.float32
BF16 = jnp.bfloat16

D_MODEL = 1024
HEAD_DIM = 64
HALF = HEAD_DIM // 2
N_HEADS = 8
WIDTH = N_HEADS * HEAD_DIM
BLK = 256
TOPK = 3
ROPE_THETA = 10000.0
D_FF = 2816
RMS_EPS = 1e-6
NEG = -1e30
SCALE = HEAD_DIM ** -0.5

QA_ROWS = 128
VA_ROWS = 80
MOBA_CH = 2
MOBA_TQ = MOBA_CH * BLK
SB_STOP = 115.0
SB_HEADS = 4
SB_QT = 4
LOG2E = float(np.log2(np.e))

MERGE_TM = 512
FFN_TM = 512
FFN_SUB = 256
BF16_ROWS = 16
HALO = BF16_ROWS

VMEM_LIMIT = 48 * 1024 * 1024


def _rms(x, g):
    ms = jnp.mean(x * x, axis=-1, keepdims=True)
    return x * lax.rsqrt(ms + RMS_EPS) * g


def _split_bf16(x):
    hi = x.astype(BF16)
    lo = (x - hi.astype(F32)).astype(BF16)
    return hi, lo


def _dot(a, b):
    return jnp.dot(a, b, preferred_element_type=F32)


def _dot_t(a_t, b):
    return lax.dot_general(a_t, b, (((0,), (0,)), ((), ())), preferred_element_type=F32)


def _proj_kernel(nb, x_ref, g_ref, wrow_ref, wt_ref, cq_ref, sq_ref, ck_ref, sk_ref,
                 qa_ref, ka_ref, va_ref, qb_ref, kb_ref, vb_ref, km_ref):
    blk = pl.program_id(0) % nb

    @pl.when(pl.program_id(0) == 0)
    def _():
        km_ref[...] = jnp.zeros_like(km_ref)

    h = _rms(x_ref[...], g_ref[...]).astype(BF16)
    rows = _dot(h, wrow_ref[...])
    cols = lax.dot_general(wt_ref[...], h, (((1,), (1,)), ((), ())),
                           preferred_element_type=F32)

    ka = rows[:, :WIDTH]
    lane = lax.broadcasted_iota(jnp.int32, (BLK, WIDTH), 1)
    first_half = (lane % HEAD_DIM) < HALF
    swapped = jnp.where(first_half, pltpu.roll(ka, WIDTH - HALF, 1), pltpu.roll(ka, HALF, 1))
    ck = jnp.concatenate([ck_ref[...]] * (WIDTH // 128), axis=1)
    sk = jnp.concatenate([sk_ref[...]] * (WIDTH // 128), axis=1)
    ka = ka * ck + swapped * sk
    km_ref[pl.ds(blk, 1), :] = jnp.mean(ka, axis=0, keepdims=True)
    onehot = (lax.broadcasted_iota(jnp.int32, (BLK, HEAD_DIM), 1) == blk).astype(F32)
    ka_blocks = []
    for hd in range(N_HEADS):
        ka_blocks += [ka[:, hd * HEAD_DIM:(hd + 1) * HEAD_DIM], onehot]
    ka_ref[0, 0] = jnp.concatenate(ka_blocks, axis=1).astype(BF16)
    kb_ref[0, 0] = rows[:, WIDTH:].astype(BF16)

    cq = cq_ref[...]
    sq = sq_ref[...]
    km = km_ref[...]
    n_iota = lax.broadcasted_iota(jnp.int32, (nb, BLK), 0).astype(F32)
    blk_f = blk.astype(F32)
    past = n_iota < blk_f
    zeros_q = jnp.zeros((QA_ROWS - HEAD_DIM - nb, BLK), F32)
    zeros_h = jnp.zeros((HEAD_DIM, BLK), F32)
    ones_v = jnp.ones((VA_ROWS - HEAD_DIM, BLK), F32)
    for hd in range(N_HEADS):
        r0 = hd * HEAD_DIM
        t1 = cols[r0:r0 + HALF]
        t2 = cols[r0 + HALF:r0 + HEAD_DIM]
        q = jnp.concatenate([t1 * cq - t2 * sq, t2 * cq + t1 * sq], axis=0)

        q_hi, q_lo = _split_bf16(q)
        km_hi, km_lo = _split_bf16(km[:, r0:r0 + HEAD_DIM])
        gate = _dot(km_hi, q_hi) + (_dot(km_hi, q_lo) + _dot(km_lo, q_hi))
        gate = jnp.where(past, gate, NEG)
        bias = jnp.where(n_iota == blk_f, 0.0, NEG)
        for r in range(TOPK):
            m = jnp.max(gate, axis=0, keepdims=True)
            idx = jnp.min(jnp.where(gate == m, n_iota, float(nb)), axis=0, keepdims=True)
            pick = n_iota == idx
            bias = jnp.where(pick, jnp.where(blk > r, 0.0, bias), bias)
            gate = jnp.where(pick, -jnp.inf, gate)
        qa_ref[0, hd] = jnp.concatenate([q, bias, zeros_q], axis=0).astype(BF16)

        va_ref[0, hd, 0] = jnp.concatenate([cols[WIDTH + r0:WIDTH + r0 + HEAD_DIM], ones_v],
                                           axis=0).astype(BF16)
        qb = cols[2 * WIDTH + r0:2 * WIDTH + r0 + HEAD_DIM] * (SCALE * LOG2E)
        qb_ref[0, hd] = jnp.concatenate([qb, zeros_h] if hd % 2 == 0 else [zeros_h, qb],
                                        axis=0).astype(BF16)
        vb_ref[0, hd, 0] = cols[3 * WIDTH + r0:3 * WIDTH + r0 + HEAD_DIM].astype(BF16)


def _proj(x2, g_mix, w_row, w_t, cq, sq, ck, sk, batch, seq):
    nb = seq // BLK
    grid = (batch * nb,)
    bi = lambda i: i // nb
    ji = lambda i: i % nb
    out_shape = (
        jax.ShapeDtypeStruct((batch, N_HEADS, QA_ROWS, seq), BF16),
        jax.ShapeDtypeStruct((batch, nb, BLK, N_HEADS * 128), BF16),
        jax.ShapeDtypeStruct((batch, N_HEADS, nb // MOBA_CH, VA_ROWS, MOBA_CH * BLK), BF16),
        jax.ShapeDtypeStruct((batch, N_HEADS, 2 * HEAD_DIM, seq), BF16),
        jax.ShapeDtypeStruct((batch, nb, BLK, WIDTH), BF16),
        jax.ShapeDtypeStruct((batch, N_HEADS, nb, HEAD_DIM, BLK), BF16),
    )
    in_specs = [
        pl.BlockSpec((BLK, D_MODEL), lambda i: (i, 0)),
        pl.BlockSpec((1, D_MODEL), lambda i: (0, 0)),
        pl.BlockSpec((D_MODEL, 2 * WIDTH), lambda i: (0, 0)),
        pl.BlockSpec((4 * WIDTH, D_MODEL), lambda i: (0, 0)),
        pl.BlockSpec((HALF, BLK), lambda i: (0, ji(i))),
        pl.BlockSpec((HALF, BLK), lambda i: (0, ji(i))),
        pl.BlockSpec((BLK, 128), lambda i: (ji(i), 0)),
        pl.BlockSpec((BLK, 128), lambda i: (ji(i), 0)),
    ]
    out_specs = (
        pl.BlockSpec((1, N_HEADS, QA_ROWS, BLK), lambda i: (bi(i), 0, 0, ji(i))),
        pl.BlockSpec((1, 1, BLK, N_HEADS * 128), lambda i: (bi(i), ji(i), 0, 0)),
        pl.BlockSpec((1, N_HEADS, 1, VA_ROWS, BLK),
                     lambda i: (bi(i), 0, ji(i) // MOBA_CH, 0, ji(i) % MOBA_CH)),
        pl.BlockSpec((1, N_HEADS, 2 * HEAD_DIM, BLK), lambda i: (bi(i), 0, 0, ji(i))),
        pl.BlockSpec((1, 1, BLK, WIDTH), lambda i: (bi(i), ji(i), 0, 0)),
        pl.BlockSpec((1, N_HEADS, 1, HEAD_DIM, BLK), lambda i: (bi(i), 0, ji(i), 0, 0)),
    )
    return pl.pallas_call(
        functools.partial(_proj_kernel, nb),
        grid=grid, in_specs=in_specs, out_specs=out_specs, out_shape=out_shape,
        scratch_shapes=[pltpu.VMEM((nb, WIDTH), F32)],
        compiler_params=pltpu.CompilerParams(dimension_semantics=("arbitrary",),
                                             vmem_limit_bytes=VMEM_LIMIT),
        name="proj",
    )(x2, g_mix, w_row, w_t, cq, sq, ck, sk)


def _moba_kernel(q_ref, qn_ref, k_ref, v_ref, o_ref, s_ref, m_ref, acc_ref, cmax_ref):
    n_past = pl.program_id(2)
    rows = MOBA_TQ
    q_pos = lax.broadcasted_iota(jnp.int32, (rows, rows), 1)
    k_pos = lax.broadcasted_iota(jnp.int32, (rows, rows), 0)

    def scores(c, slot, q=q_ref):
        for hh in range(2):
            cmax = None
            for u in range(MOBA_CH):
                s = _dot(k_ref[0, c * MOBA_CH + u, :, hh * 128:(hh + 1) * 128], q[0, hh])
                s_ref[slot, hh, u * BLK:(u + 1) * BLK, :] = s
                smax = jnp.max(s, axis=0, keepdims=True)
                cmax = smax if cmax is None else jnp.maximum(cmax, smax)
            cmax_ref[slot, hh] = cmax

    def attend(c, slot, own):
        for hh in range(2):
            s = s_ref[slot, hh]
            if own:
                s = jnp.where(k_pos <= q_pos, s, NEG)
                smax = jnp.max(s, axis=0, keepdims=True)
            else:
                smax = cmax_ref[slot, hh]
            m = m_ref[hh]
            m_new = jnp.maximum(m, smax)
            p = jnp.exp2(s - m_new).astype(BF16)
            acc_ref[hh] = acc_ref[hh] * jnp.exp2(m - m_new) + _dot(v_ref[0, hh, c], p)
            m_ref[hh] = m_new

    m_ref[...] = jnp.full_like(m_ref, NEG)
    acc_ref[...] = jnp.zeros_like(acc_ref)
    @pl.when(n_past == 0)
    def _():
        scores(0, 0)

    def next_tile_scores():
        scores(0, 0, qn_ref)

    def body(j, carry):
        c = 3 * j
        scores(c + 1, 1)
        attend(c, 0, False)
        scores(c + 2, 2)
        attend(c + 1, 1, False)
        scores(c + 3, 0)
        attend(c + 2, 2, False)
        return carry

    n_loop = n_past // 3
    lax.fori_loop(0, n_loop, body, 0)
    first = 3 * n_loop
    left = n_past - first

    @pl.when(left == 0)
    def _():
        attend(n_past, 0, True)
        next_tile_scores()

    @pl.when(left == 1)
    def _():
        scores(n_past, 1)
        attend(first, 0, False)
        next_tile_scores()
        attend(n_past, 1, True)

    @pl.when(left == 2)
    def _():
        scores(first + 1, 1)
        attend(first, 0, False)
        scores(n_past, 2)
        attend(first + 1, 1, False)
        next_tile_scores()
        attend(n_past, 2, True)

    outs = []
    for hh in range(2):
        acc = acc_ref[hh]
        outs.append(acc[:HEAD_DIM] / acc[HEAD_DIM:HEAD_DIM + 1])
    o_ref[0] = jnp.concatenate(outs, axis=0).astype(BF16)


def _moba(qa, ka, va, batch, seq):
    nb = seq // BLK
    nc = seq // MOBA_TQ
    return pl.pallas_call(
        _moba_kernel,
        grid=(batch, N_HEADS // 2, nc),
        in_specs=[
            pl.BlockSpec((1, 2, QA_ROWS, MOBA_TQ), lambda b, p, i: (b, p, 0, i)),
            pl.BlockSpec((1, 2, QA_ROWS, MOBA_TQ), lambda b, p, i: (b, p, 0, jnp.minimum(i + 1, nc - 1))),
            pl.BlockSpec((1, nb, BLK, 256), lambda b, p, i: (b, 0, 0, p)),
            pl.BlockSpec((1, 2, nc, VA_ROWS, MOBA_TQ), lambda b, p, i: (b, p, 0, 0, 0)),
        ],
        out_specs=pl.BlockSpec((1, 2 * HEAD_DIM, MOBA_TQ), lambda b, p, i: (b, p, i)),
        out_shape=jax.ShapeDtypeStruct((batch, WIDTH, seq), BF16),
        scratch_shapes=[pltpu.VMEM((3, 2, MOBA_TQ, MOBA_TQ), F32),
                        pltpu.VMEM((2, 1, MOBA_TQ), F32),
                        pltpu.VMEM((2, VA_ROWS, MOBA_TQ), F32),
                        pltpu.VMEM((3, 2, 1, MOBA_TQ), F32)],
        compiler_params=pltpu.CompilerParams(
            dimension_semantics=("arbitrary", "arbitrary", "arbitrary"),
            vmem_limit_bytes=VMEM_LIMIT),
        name="moba",
    )(qa, qa, ka, va)


def _sb_kernel(qall_ref, k_ref, v_ref, o_ref, run_all, acc_all):
    i0 = pl.program_id(2) * SB_QT
    k_iota = lax.broadcasted_iota(jnp.int32, (BLK, BLK), 0)
    q_iota = lax.broadcasted_iota(jnp.int32, (BLK, BLK), 1)
    strict = k_iota < q_iota
    later = (q_iota > k_iota).astype(BF16)
    later2 = jnp.concatenate([later, later], axis=1)

    def pair_step(j, diag, q_ref, run_ref, acc_ref):
        has_old = j >= 1
        tiles = (j, jnp.maximum(j - 1, 0))
        heads = range(SB_HEADS)
        z, sp, w = {}, {}, {}
        for hh in heads:
            lanes = slice((hh // 2) * 128, (hh // 2 + 1) * 128)
            for u in range(2):
                z[hh, u] = _dot(k_ref[0, tiles[u], :, lanes], q_ref[0, hh])
        for hh in heads:
            for u in range(2):
                zz = z[hh, u]
                s = jnp.maximum(zz, 0.0) + jnp.log2(1.0 + jnp.exp2(-jnp.abs(zz)))
                if diag and u == 0:
                    s = jnp.where(strict, s, 0.0)
                sp[hh, u] = s
                w[hh, u] = jnp.concatenate(_split_bf16(s), axis=0)
        aft = {key: _dot(later2, w[key]) for key in w}
        for hh in heads:
            run = run_ref[hh]
            pv = None
            for u in range(2):
                if u == 1:
                    run = jnp.where(has_old, run, NEG)
                a = jnp.exp2((z[hh, u] - sp[hh, u]) - aft[hh, u] + run)
                if diag and u == 0:
                    a = jnp.where(strict, a, 0.0)
                part = _dot(v_ref[0, hh, tiles[u]], a.astype(BF16))
                pv = part if pv is None else pv + part
                run = run - (aft[hh, u][0:1] + sp[hh, u][0:1])
            acc_ref[hh] += pv
            run_ref[hh] = run

    run_all[...] = jnp.zeros_like(run_all)
    acc_all[...] = jnp.zeros_like(acc_all)
    views = [(qall_ref.at[:, :, :, t * BLK:(t + 1) * BLK], run_all.at[t], acc_all.at[t])
             for t in range(SB_QT)]
    for t in range(SB_QT):
        pair_step(i0 + t, True, *views[t])
    for t in range(SB_QT):
        def cond(j, t=t):
            return jnp.logical_and(j >= 0, jnp.max(run_all[t]) > -SB_STOP * LOG2E)

        def body(j, t=t):
            pair_step(j, False, *views[t])
            return j - 2

        lax.while_loop(cond, body, i0 + t - 2)
    o_ref[0] = jnp.concatenate(
        [jnp.concatenate([acc_all[t, hh] for hh in range(SB_HEADS)], axis=0) for t in range(SB_QT)],
        axis=1).astype(BF16)


def _sb(qb, kb, vb, batch, seq):
    nb = seq // BLK
    return pl.pallas_call(
        _sb_kernel,
        grid=(batch, N_HEADS // SB_HEADS, nb // SB_QT),
        in_specs=[
            pl.BlockSpec((1, SB_HEADS, 2 * HEAD_DIM, SB_QT * BLK), lambda b, g, i: (b, g, 0, i)),
            pl.BlockSpec((1, nb, BLK, SB_HEADS * HEAD_DIM), lambda b, g, i: (b, 0, 0, g)),
            pl.BlockSpec((1, SB_HEADS, nb, HEAD_DIM, BLK), lambda b, g, i: (b, g, 0, 0, 0)),
        ],
        out_specs=pl.BlockSpec((1, SB_HEADS * HEAD_DIM, SB_QT * BLK), lambda b, g, i: (b, g, i)),
        out_shape=jax.ShapeDtypeStruct((batch, WIDTH, seq), BF16),
        scratch_shapes=[pltpu.VMEM((SB_QT, SB_HEADS, 1, BLK), F32),
                        pltpu.VMEM((SB_QT, SB_HEADS, HEAD_DIM, BLK), F32)],
        compiler_params=pltpu.CompilerParams(
            dimension_semantics=("arbitrary", "arbitrary", "arbitrary"),
            vmem_limit_bytes=VMEM_LIMIT),
        name="sb",
    )(qb, kb, vb)


def _merge_kernel(x_ref, ya_ref, yb_ref, gm_ref, wg_ref, bg_ref, wa_ref, wb_ref, wo_ref, gf_ref,
                  wu32_ref, wd32_ref, x1_ref, h2_ref, wu16_ref, wd16_ref):
    wu16_ref[...] = wu32_ref[...].astype(BF16)
    wd16_ref[...] = wd32_ref[...].astype(BF16)
    x = x_ref[...]
    h = _rms(x, gm_ref[...]).astype(BF16)
    gates = jax.nn.sigmoid(_dot(h, wg_ref[...]) + bg_ref[...])
    merged = (gates[:, :D_MODEL] * _dot_t(ya_ref[0], wa_ref[...])
              + gates[:, D_MODEL:] * _dot_t(yb_ref[0], wb_ref[...]))
    x1 = x + _dot(merged.astype(BF16), wo_ref[...])
    x1_ref[...] = x1
    h2_ref[...] = _rms(x1, gf_ref[...]).astype(BF16)


def _slab_spec(n_rows, width, steps):
    n = max(d for d in range(1, steps + 1) if n_rows % d == 0 and (n_rows // d) % BF16_ROWS == 0)
    return pl.BlockSpec((n_rows // n, width), lambda i: (jnp.minimum(i, n - 1), 0))


def _merge(x2, ya_t, yb_t, g_mix, w_gate, b_gate, w_a, w_b, w_out, g_ffn, w_up, w_down):
    t = x2.shape[0]
    tm = MERGE_TM
    steps = t // tm
    tiles_per_seq = ya_t.shape[2] // tm
    full = lambda shape: pl.BlockSpec(shape, lambda i: (0, 0))
    y_spec = pl.BlockSpec((1, WIDTH, tm), lambda i: (i // tiles_per_seq, 0, i % tiles_per_seq))
    wu_spec = _slab_spec(D_MODEL, 2 * D_FF, steps)
    wd_spec = _slab_spec(D_FF, D_MODEL, steps)
    return pl.pallas_call(
        _merge_kernel,
        grid=(steps,),
        in_specs=[
            pl.BlockSpec((tm, D_MODEL), lambda i: (i, 0)),
            y_spec,
            y_spec,
            full((1, D_MODEL)),
            full((D_MODEL, 2 * D_MODEL)),
            full((1, 2 * D_MODEL)),
            full((WIDTH, D_MODEL)),
            full((WIDTH, D_MODEL)),
            full((D_MODEL, D_MODEL)),
            full((1, D_MODEL)),
            wu_spec,
            wd_spec,
        ],
        out_specs=(pl.BlockSpec((tm, D_MODEL), lambda i: (i, 0)),
                   pl.BlockSpec((tm, D_MODEL), lambda i: (i, 0)),
                   wu_spec,
                   wd_spec),
        out_shape=(jax.ShapeDtypeStruct((t, D_MODEL), F32),
                   jax.ShapeDtypeStruct((t, D_MODEL), BF16),
                   jax.ShapeDtypeStruct(w_up.shape, BF16),
                   jax.ShapeDtypeStruct(w_down.shape, BF16)),
        compiler_params=pltpu.CompilerParams(dimension_semantics=("arbitrary",),
                                             vmem_limit_bytes=VMEM_LIMIT),
        name="merge",
    )(x2, ya_t, yb_t, g_mix, w_gate, b_gate, w_a, w_b, w_out, g_ffn, w_up, w_down)


def _ffn_kernel(tiles_per_seq, h_ref, halo_ref, x1_ref, wu_ref, cw_ref, cb_ref, wd_ref, gfin_ref,
                o_ref, act_ref):
    i = pl.program_id(0)
    halo = jnp.where(i % tiles_per_seq == 0, jnp.zeros_like(halo_ref[...]), halo_ref[...])
    hx = jnp.concatenate([halo, h_ref[...]], axis=0)

    for s in range(D_FF // FFN_SUB):
        conv = []
        for half in range(2):
            col = half * D_FF + s * FFN_SUB
            u = _dot(hx, wu_ref[:, col:col + FFN_SUB])
            cw = cw_ref[:, col:col + FFN_SUB]
            taps = pltpu.roll(u, 2, 0) * cw[0:1] + pltpu.roll(u, 1, 0) * cw[1:2] + u * cw[2:3]
            conv.append(taps[HALO:] + cb_ref[:, col:col + FFN_SUB])
        ug, uv = conv
        act_ref[:, s * FFN_SUB:(s + 1) * FFN_SUB] = (ug * jax.nn.sigmoid(ug) * uv).astype(BF16)
    y = x1_ref[...] + _dot(act_ref[...], wd_ref[...])
    o_ref[...] = _rms(y, gfin_ref[...])


def _ffn(h2, x1, w_up, conv_w, conv_b, w_down, g_final, seq):
    t = h2.shape[0]
    tm = FFN_TM
    hb = tm // HALO
    resident = lambda shape: pl.BlockSpec(shape, lambda i: (0, 0), pipeline_mode=pl.Buffered(1))
    return pl.pallas_call(
        functools.partial(_ffn_kernel, seq // tm),
        grid=(t // tm,),
        in_specs=[
            pl.BlockSpec((tm, D_MODEL), lambda i: (i, 0)),
            pl.BlockSpec((HALO, D_MODEL), lambda i: (jnp.maximum(i * hb - 1, 0), 0)),
            pl.BlockSpec((tm, D_MODEL), lambda i: (i, 0)),
            resident((D_MODEL, 2 * D_FF)),
            resident((TOPK, 2 * D_FF)),
            resident((1, 2 * D_FF)),
            resident((D_FF, D_MODEL)),
            resident((1, D_MODEL)),
        ],
        out_specs=pl.BlockSpec((tm, D_MODEL), lambda i: (i, 0)),
        out_shape=jax.ShapeDtypeStruct((t, D_MODEL), F32),
        scratch_shapes=[pltpu.VMEM((tm, D_FF), BF16)],
        compiler_params=pltpu.CompilerParams(dimension_semantics=("arbitrary",),
                                             vmem_limit_bytes=VMEM_LIMIT),
        name="ffn",
    )(h2, h2, x1, w_up, conv_w, conv_b, w_down, g_final)


def _rope_tables(seq):
    inv = ROPE_THETA ** (-np.arange(HALF, dtype=np.float64) / HALF)
    ang = np.arange(seq, dtype=np.float64)[:, None] * inv[None, :]
    cos, sin = np.cos(ang), np.sin(ang)
    ck = np.concatenate([cos, cos, cos, cos], axis=1)
    sk = np.concatenate([-sin, sin, -sin, sin], axis=1)
    q_scale = SCALE * LOG2E
    return tuple(jnp.asarray(t, F32) for t in (cos.T * q_scale, sin.T * q_scale, ck, sk))


def kernel(x, g_mix, w_in, b_gate, w_branch_a, w_branch_b, w_out, g_ffn, w_up, conv_w, conv_b,
           w_down, g_final):
    batch, seq, _ = x.shape
    assert seq % BLK == 0 and seq % FFN_TM == 0 and seq // BLK <= HALF
    assert g_mix.shape[0] == 1
    x2 = x.reshape(batch * seq, D_MODEL)
    cq, sq, ck, sk = _rope_tables(seq)
    for layer in range(1):
        w = w_in[layer]
        col = lambda k: w[:, k * WIDTH:(k + 1) * WIDTH]
        w_row = jnp.concatenate([col(1), col(4)], axis=1).astype(BF16)
        w_t = jnp.concatenate([col(0), col(2), col(3), col(5)], axis=1).T.astype(BF16)
        w_gate = w[:, 6 * WIDTH:].astype(BF16)
        gm = g_mix[layer][None, :]
        qa, ka, va, qb, kb, vb = _proj(x2, gm, w_row, w_t, cq, sq, ck, sk, batch, seq)
        ya_t = _moba(qa, ka, va, batch, seq)
        yb_t = _sb(qb, kb, vb, batch, seq)
        x1, h2, w_up16, w_down16 = _merge(
            x2, ya_t, yb_t, gm, w_gate, b_gate[layer][None, :],
            w_branch_a[layer].astype(BF16), w_branch_b[layer].astype(BF16),
            w_out[layer].astype(BF16), g_ffn[layer][None, :], w_up[layer], w_down[layer])
        x2 = _ffn(h2, x1, w_up16, conv_w[layer], conv_b[layer][None, :], w_down16,
                  g_final[None, :], seq)
    return x2.reshape(batch, seq, D_MODEL)
```

```python
import functools

import numpy as np
import jax
import jax.numpy as jnp
from jax import lax
from jax.experimental import pallas as pl
from jax.experimental.pallas import tpu as pltpu

F32 = jnp.float32
BF16 = jnp.bfloat16

D_MODEL = 1024
HEAD_DIM = 64
HALF = HEAD_DIM // 2
N_HEADS = 8
WIDTH = N_HEADS * HEAD_DIM
BLK = 256
TOPK = 3
ROPE_THETA = 10000.0
D_FF = 2816
RMS_EPS = 1e-6
NEG = -1e30
SCALE = HEAD_DIM ** -0.5

LANES = 128
PAIR = 2 * HEAD_DIM
assert PAIR == LANES
QA_ROWS = LANES
VA_ROWS = 80
MOBA_CH = 2
MOBA_TQ = MOBA_CH * BLK
SB_STOP = 115.0
SB_HEADS = 4
SB_QT = 4
LOG2E = float(np.log2(np.e))

MERGE_TM = 512
FFN_TM = 512
FFN_SUB = 256
BF16_ROWS = 16
HALO = BF16_ROWS

VMEM_LIMIT = 48 * 1024 * 1024


def _rms(x, g):
    ms = jnp.mean(x * x, axis=-1, keepdims=True)
    return x * lax.rsqrt(ms + RMS_EPS) * g


def _split_bf16(x):
    hi = x.astype(BF16)
    lo = (x - hi.astype(F32)).astype(BF16)
    return hi, lo


def _dot(a, b):
    return jnp.dot(a, b, preferred_element_type=F32)


def _dot_t(a_t, b):
    return lax.dot_general(a_t, b, (((0,), (0,)), ((), ())), preferred_element_type=F32)


def _proj_kernel(nb, x_ref, g_ref, wrow_ref, wt_ref, cq_ref, sq_ref, ck_ref, sk_ref,
                 qa_ref, ka_ref, va_ref, qb_ref, kb_ref, vb_ref, km_ref):
    blk = pl.program_id(0) % nb

    @pl.when(pl.program_id(0) == 0)
    def _():
        km_ref[...] = jnp.zeros_like(km_ref)

    h = _rms(x_ref[...], g_ref[...]).astype(BF16)
    rows = _dot(h, wrow_ref[...])
    cols = lax.dot_general(wt_ref[...], h, (((1,), (1,)), ((), ())),
                           preferred_element_type=F32)

    ka = rows[:, :WIDTH]
    lane = lax.broadcasted_iota(jnp.int32, (BLK, WIDTH), 1)
    first_half = (lane % HEAD_DIM) < HALF
    swapped = jnp.where(first_half, pltpu.roll(ka, WIDTH - HALF, 1), pltpu.roll(ka, HALF, 1))
    ck = jnp.concatenate([ck_ref[...]] * (WIDTH // PAIR), axis=1)
    sk = jnp.concatenate([sk_ref[...]] * (WIDTH // PAIR), axis=1)
    ka = ka * ck + swapped * sk
    km_ref[pl.ds(blk, 1), :] = jnp.mean(ka, axis=0, keepdims=True)
    onehot = (lax.broadcasted_iota(jnp.int32, (BLK, HEAD_DIM), 1) == blk).astype(F32)
    ka_blocks = []
    for hd in range(N_HEADS):
        ka_blocks += [ka[:, hd * HEAD_DIM:(hd + 1) * HEAD_DIM], onehot]
    ka_ref[0, 0] = jnp.concatenate(ka_blocks, axis=1).astype(BF16)
    kb_ref[0, 0] = rows[:, WIDTH:].astype(BF16)

    cq = cq_ref[...]
    sq = sq_ref[...]
    km = km_ref[...]
    n_iota = lax.broadcasted_iota(jnp.int32, (nb, BLK), 0).astype(F32)
    blk_f = blk.astype(F32)
    past = n_iota < blk_f
    zeros_q = jnp.zeros((QA_ROWS - HEAD_DIM - nb, BLK), F32)
    zeros_h = jnp.zeros((HEAD_DIM, BLK), F32)
    ones_v = jnp.ones((VA_ROWS - HEAD_DIM, BLK), F32)
    for hd in range(N_HEADS):
        r0 = hd * HEAD_DIM
        t1 = cols[r0:r0 + HALF]
        t2 = cols[r0 + HALF:r0 + HEAD_DIM]
        q = jnp.concatenate([t1 * cq - t2 * sq, t2 * cq + t1 * sq], axis=0)

        q_hi, q_lo = _split_bf16(q)
        km_hi, km_lo = _split_bf16(km[:, r0:r0 + HEAD_DIM])
        gate = _dot(km_hi, q_hi) + (_dot(km_hi, q_lo) + _dot(km_lo, q_hi))
        gate = jnp.where(past, gate, NEG)
        bias = jnp.where(n_iota == blk_f, 0.0, NEG)
        for r in range(TOPK):
            m = jnp.max(gate, axis=0, keepdims=True)
            idx = jnp.min(jnp.where(gate == m, n_iota, float(nb)), axis=0, keepdims=True)
            pick = n_iota == idx
            bias = jnp.where(pick, jnp.where(blk > r, 0.0, bias), bias)
            gate = jnp.where(pick, -jnp.inf, gate)
        qa_ref[0, hd] = jnp.concatenate([q, bias, zeros_q], axis=0).astype(BF16)

        va_ref[0, hd, 0] = jnp.concatenate([cols[WIDTH + r0:WIDTH + r0 + HEAD_DIM], ones_v],
                                           axis=0).astype(BF16)
        qb = cols[2 * WIDTH + r0:2 * WIDTH + r0 + HEAD_DIM] * (SCALE * LOG2E)
        qb_ref[0, hd] = jnp.concatenate([qb, zeros_h] if hd % 2 == 0 else [zeros_h, qb],
                                        axis=0).astype(BF16)
        vb_ref[0, hd, 0] = cols[3 * WIDTH + r0:3 * WIDTH + r0 + HEAD_DIM].astype(BF16)


def _proj(x2, g_mix, w_row, w_t, cq, sq, ck, sk, batch, seq):
    nb = seq // BLK
    grid = (batch * nb,)
    bi = lambda i: i // nb
    ji = lambda i: i % nb
    out_shape = (
        jax.ShapeDtypeStruct((batch, N_HEADS, QA_ROWS, seq), BF16),
        jax.ShapeDtypeStruct((batch, nb, BLK, N_HEADS * QA_ROWS), BF16),
        jax.ShapeDtypeStruct((batch, N_HEADS, nb // MOBA_CH, VA_ROWS, MOBA_CH * BLK), BF16),
        jax.ShapeDtypeStruct((batch, N_HEADS, 2 * HEAD_DIM, seq), BF16),
        jax.ShapeDtypeStruct((batch, nb, BLK, WIDTH), BF16),
        jax.ShapeDtypeStruct((batch, N_HEADS, nb, HEAD_DIM, BLK), BF16),
    )
    in_specs = [
        pl.BlockSpec((BLK, D_MODEL), lambda i: (i, 0)),
        pl.BlockSpec((1, D_MODEL), lambda i: (0, 0)),
        pl.BlockSpec((D_MODEL, 2 * WIDTH), lambda i: (0, 0)),
        pl.BlockSpec((4 * WIDTH, D_MODEL), lambda i: (0, 0)),
        pl.BlockSpec((HALF, BLK), lambda i: (0, ji(i))),
        pl.BlockSpec((HALF, BLK), lambda i: (0, ji(i))),
        pl.BlockSpec((BLK, PAIR), lambda i: (ji(i), 0)),
        pl.BlockSpec((BLK, PAIR), lambda i: (ji(i), 0)),
    ]
    out_specs = (
        pl.BlockSpec((1, N_HEADS, QA_ROWS, BLK), lambda i: (bi(i), 0, 0, ji(i))),
        pl.BlockSpec((1, 1, BLK, N_HEADS * QA_ROWS), lambda i: (bi(i), ji(i), 0, 0)),
        pl.BlockSpec((1, N_HEADS, 1, VA_ROWS, BLK),
                     lambda i: (bi(i), 0, ji(i) // MOBA_CH, 0, ji(i) % MOBA_CH)),
        pl.BlockSpec((1, N_HEADS, 2 * HEAD_DIM, BLK), lambda i: (bi(i), 0, 0, ji(i))),
        pl.BlockSpec((1, 1, BLK, WIDTH), lambda i: (bi(i), ji(i), 0, 0)),
        pl.BlockSpec((1, N_HEADS, 1, HEAD_DIM, BLK), lambda i: (bi(i), 0, ji(i), 0, 0)),
    )
    return pl.pallas_call(
        functools.partial(_proj_kernel, nb),
        grid=grid, in_specs=in_specs, out_specs=out_specs, out_shape=out_shape,
        scratch_shapes=[pltpu.VMEM((nb, WIDTH), F32)],
        compiler_params=pltpu.CompilerParams(dimension_semantics=("arbitrary",),
                                             vmem_limit_bytes=VMEM_LIMIT),
        name="proj",
    )(x2, g_mix, w_row, w_t, cq, sq, ck, sk)


def _moba_kernel(q_ref, qn_ref, k_ref, v_ref, o_ref, s_ref, m_ref, acc_ref, cmax_ref):
    n_past = pl.program_id(2)
    rows = MOBA_TQ
    q_pos = lax.broadcasted_iota(jnp.int32, (rows, rows), 1)
    k_pos = lax.broadcasted_iota(jnp.int32, (rows, rows), 0)

    def scores(c, slot, q=q_ref):
        for hh in range(2):
            cmax = None
            for u in range(MOBA_CH):
                s = _dot(k_ref[0, c * MOBA_CH + u, :, hh * QA_ROWS:(hh + 1) * QA_ROWS], q[0, hh])
                s_ref[slot, hh, u * BLK:(u + 1) * BLK, :] = s
                smax = jnp.max(s, axis=0, keepdims=True)
                cmax = smax if cmax is None else jnp.maximum(cmax, smax)
            cmax_ref[slot, hh] = cmax

    def attend(c, slot, own):
        for hh in range(2):
            s = s_ref[slot, hh]
            if own:
                s = jnp.where(k_pos <= q_pos, s, NEG)
                smax = jnp.max(s, axis=0, keepdims=True)
            else:
                smax = cmax_ref[slot, hh]
            m = m_ref[hh]
            m_new = jnp.maximum(m, smax)
            p = jnp.exp2(s - m_new).astype(BF16)
            acc_ref[hh] = acc_ref[hh] * jnp.exp2(m - m_new) + _dot(v_ref[0, hh, c], p)
            m_ref[hh] = m_new

    m_ref[...] = jnp.full_like(m_ref, NEG)
    acc_ref[...] = jnp.zeros_like(acc_ref)
    @pl.when(n_past == 0)
    def _():
        scores(0, 0)

    def next_tile_scores():
        scores(0, 0, qn_ref)

    def body(j, carry):
        c = 3 * j
        scores(c + 1, 1)
        attend(c, 0, False)
        scores(c + 2, 2)
        attend(c + 1, 1, False)
        scores(c + 3, 0)
        attend(c + 2, 2, False)
        return carry

    n_loop = n_past // 3
    lax.fori_loop(0, n_loop, body, 0)
    first = 3 * n_loop
    left = n_past - first

    @pl.when(left == 0)
    def _():
        attend(n_past, 0, True)
        next_tile_scores()

    @pl.when(left == 1)
    def _():
        scores(n_past, 1)
        attend(first, 0, False)
        next_tile_scores()
        attend(n_past, 1, True)

    @pl.when(left == 2)
    def _():
        scores(first + 1, 1)
        attend(first, 0, False)
        scores(n_past, 2)
        attend(first + 1, 1, False)
        next_tile_scores()
        attend(n_past, 2, True)

    outs = []
    for hh in range(2):
        acc = acc_ref[hh]
        outs.append(acc[:HEAD_DIM] / acc[HEAD_DIM:HEAD_DIM + 1])
    o_ref[0] = jnp.concatenate(outs, axis=0).astype(BF16)


def _moba(qa, ka, va, batch, seq):
    nb = seq // BLK
    nc = seq // MOBA_TQ
    return pl.pallas_call(
        _moba_kernel,
        grid=(batch, N_HEADS // 2, nc),
        in_specs=[
            pl.BlockSpec((1, 2, QA_ROWS, MOBA_TQ), lambda b, p, i: (b, p, 0, i)),
            pl.BlockSpec((1, 2, QA_ROWS, MOBA_TQ), lambda b, p, i: (b, p, 0, jnp.minimum(i + 1, nc - 1))),
            pl.BlockSpec((1, nb, BLK, 2 * QA_ROWS), lambda b, p, i: (b, 0, 0, p)),
            pl.BlockSpec((1, 2, nc, VA_ROWS, MOBA_TQ), lambda b, p, i: (b, p, 0, 0, 0)),
        ],
        out_specs=pl.BlockSpec((1, 2 * HEAD_DIM, MOBA_TQ), lambda b, p, i: (b, p, i)),
        out_shape=jax.ShapeDtypeStruct((batch, WIDTH, seq), BF16),
        scratch_shapes=[pltpu.VMEM((3, 2, MOBA_TQ, MOBA_TQ), F32),
                        pltpu.VMEM((2, 1, MOBA_TQ), F32),
                        pltpu.VMEM((2, VA_ROWS, MOBA_TQ), F32),
                        pltpu.VMEM((3, 2, 1, MOBA_TQ), F32)],
        compiler_params=pltpu.CompilerParams(
            dimension_semantics=("arbitrary", "arbitrary", "arbitrary"),
            vmem_limit_bytes=VMEM_LIMIT),
        name="moba",
    )(qa, qa, ka, va)


def _sb_kernel(qall_ref, k_ref, v_ref, o_ref, run_all, acc_all):
    i0 = pl.program_id(2) * SB_QT
    k_iota = lax.broadcasted_iota(jnp.int32, (BLK, BLK), 0)
    q_iota = lax.broadcasted_iota(jnp.int32, (BLK, BLK), 1)
    strict = k_iota < q_iota
    later = (q_iota > k_iota).astype(BF16)
    later2 = jnp.concatenate([later, later], axis=1)

    def pair_step(j, diag, q_ref, run_ref, acc_ref):
        has_old = j >= 1
        tiles = (j, jnp.maximum(j - 1, 0))
        heads = range(SB_HEADS)
        z, sp, w = {}, {}, {}
        for hh in heads:
            lanes = slice((hh // 2) * PAIR, (hh // 2 + 1) * PAIR)
            for u in range(2):
                z[hh, u] = _dot(k_ref[0, tiles[u], :, lanes], q_ref[0, hh])
        for hh in heads:
            for u in range(2):
                zz = z[hh, u]
                s = jnp.maximum(zz, 0.0) + jnp.log2(1.0 + jnp.exp2(-jnp.abs(zz)))
                if diag and u == 0:
                    s = jnp.where(strict, s, 0.0)
                sp[hh, u] = s
                w[hh, u] = jnp.concatenate(_split_bf16(s), axis=0)
        aft = {key: _dot(later2, w[key]) for key in w}
        for hh in heads:
            run = run_ref[hh]
            pv = None
            for u in range(2):
                if u == 1:
                    run = jnp.where(has_old, run, NEG)
                a = jnp.exp2((z[hh, u] - sp[hh, u]) - aft[hh, u] + run)
                if diag and u == 0:
                    a = jnp.where(strict, a, 0.0)
                part = _dot(v_ref[0, hh, tiles[u]], a.astype(BF16))
                pv = part if pv is None else pv + part
                run = run - (aft[hh, u][0:1] + sp[hh, u][0:1])
            acc_ref[hh] += pv
            run_ref[hh] = run

    run_all[...] = jnp.zeros_like(run_all)
    acc_all[...] = jnp.zeros_like(acc_all)
    views = [(qall_ref.at[:, :, :, t * BLK:(t + 1) * BLK], run_all.at[t], acc_all.at[t])
             for t in range(SB_QT)]
    for t in range(SB_QT):
        pair_step(i0 + t, True, *views[t])
    for t in range(SB_QT):
        def cond(j, t=t):
            return jnp.logical_and(j >= 0, jnp.max(run_all[t]) > -SB_STOP * LOG2E)

        def body(j, t=t):
            pair_step(j, False, *views[t])
            return j - 2

        lax.while_loop(cond, body, i0 + t - 2)
    o_ref[0] = jnp.concatenate(
        [jnp.concatenate([acc_all[t, hh] for hh in range(SB_HEADS)], axis=0) for t in range(SB_QT)],
        axis=1).astype(BF16)


def _sb(qb, kb, vb, batch, seq):
    nb = seq // BLK
    return pl.pallas_call(
        _sb_kernel,
        grid=(batch, N_HEADS // SB_HEADS, nb // SB_QT),
        in_specs=[
            pl.BlockSpec((1, SB_HEADS, 2 * HEAD_DIM, SB_QT * BLK), lambda b, g, i: (b, g, 0, i)),
            pl.BlockSpec((1, nb, BLK, SB_HEADS * HEAD_DIM), lambda b, g, i: (b, 0, 0, g)),
            pl.BlockSpec((1, SB_HEADS, nb, HEAD_DIM, BLK), lambda b, g, i: (b, g, 0, 0, 0)),
        ],
        out_specs=pl.BlockSpec((1, SB_HEADS * HEAD_DIM, SB_QT * BLK), lambda b, g, i: (b, g, i)),
        out_shape=jax.ShapeDtypeStruct((batch, WIDTH, seq), BF16),
        scratch_shapes=[pltpu.VMEM((SB_QT, SB_HEADS, 1, BLK), F32),
                        pltpu.VMEM((SB_QT, SB_HEADS, HEAD_DIM, BLK), F32)],
        compiler_params=pltpu.CompilerParams(
            dimension_semantics=("arbitrary", "arbitrary", "arbitrary"),
            vmem_limit_bytes=VMEM_LIMIT),
        name="sb",
    )(qb, kb, vb)


def _merge_kernel(x_ref, ya_ref, yb_ref, gm_ref, wg_ref, bg_ref, wa_ref, wb_ref, wo_ref, gf_ref,
                  wu32_ref, wd32_ref, x1_ref, h2_ref, wu16_ref, wd16_ref):
    wu16_ref[...] = wu32_ref[...].astype(BF16)
    wd16_ref[...] = wd32_ref[...].astype(BF16)
    x = x_ref[...]
    h = _rms(x, gm_ref[...]).astype(BF16)
    gates = jax.nn.sigmoid(_dot(h, wg_ref[...]) + bg_ref[...])
    merged = (gates[:, :D_MODEL] * _dot_t(ya_ref[0], wa_ref[...])
              + gates[:, D_MODEL:] * _dot_t(yb_ref[0], wb_ref[...]))
    x1 = x + _dot(merged.astype(BF16), wo_ref[...])
    x1_ref[...] = x1
    h2_ref[...] = _rms(x1, gf_ref[...]).astype(BF16)


def _slab_spec(n_rows, width, steps):
    n = max(d for d in range(1, steps + 1) if n_rows % d == 0 and (n_rows // d) % BF16_ROWS == 0)
    return pl.BlockSpec((n_rows // n, width), lambda i: (jnp.minimum(i, n - 1), 0))


def _merge(x2, ya_t, yb_t, g_mix, w_gate, b_gate, w_a, w_b, w_out, g_ffn, w_up, w_down):
    t = x2.shape[0]
    tm = MERGE_TM
    steps = t // tm
    tiles_per_seq = ya_t.shape[2] // tm
    full = lambda shape: pl.BlockSpec(shape, lambda i: (0, 0))
    y_spec = pl.BlockSpec((1, WIDTH, tm), lambda i: (i // tiles_per_seq, 0, i % tiles_per_seq))
    wu_spec = _slab_spec(D_MODEL, 2 * D_FF, steps)
    wd_spec = _slab_spec(D_FF, D_MODEL, steps)
    return pl.pallas_call(
        _merge_kernel,
        grid=(steps,),
        in_specs=[
            pl.BlockSpec((tm, D_MODEL), lambda i: (i, 0)),
            y_spec,
            y_spec,
            full((1, D_MODEL)),
            full((D_MODEL, 2 * D_MODEL)),
            full((1, 2 * D_MODEL)),
            full((WIDTH, D_MODEL)),
            full((WIDTH, D_MODEL)),
            full((D_MODEL, D_MODEL)),
            full((1, D_MODEL)),
            wu_spec,
            wd_spec,
        ],
        out_specs=(pl.BlockSpec((tm, D_MODEL), lambda i: (i, 0)),
                   pl.BlockSpec((tm, D_MODEL), lambda i: (i, 0)),
                   wu_spec,
                   wd_spec),
        out_shape=(jax.ShapeDtypeStruct((t, D_MODEL), F32),
                   jax.ShapeDtypeStruct((t, D_MODEL), BF16),
                   jax.ShapeDtypeStruct(w_up.shape, BF16),
                   jax.ShapeDtypeStruct(w_down.shape, BF16)),
        compiler_params=pltpu.CompilerParams(dimension_semantics=("arbitrary",),
                                             vmem_limit_bytes=VMEM_LIMIT),
        name="merge",
    )(x2, ya_t, yb_t, g_mix, w_gate, b_gate, w_a, w_b, w_out, g_ffn, w_up, w_down)


def _ffn_kernel(tiles_per_seq, h_ref, halo_ref, x1_ref, wu_ref, cw_ref, cb_ref, wd_ref, gfin_ref,
                o_ref, act_ref):
    i = pl.program_id(0)
    halo = jnp.where(i % tiles_per_seq == 0, jnp.zeros_like(halo_ref[...]), halo_ref[...])
    hx = jnp.concatenate([halo, h_ref[...]], axis=0)

    for s in range(D_FF // FFN_SUB):
        conv = []
        for half in range(2):
            col = half * D_FF + s * FFN_SUB
            u = _dot(hx, wu_ref[:, col:col + FFN_SUB])
            cw = cw_ref[:, col:col + FFN_SUB]
            taps = pltpu.roll(u, 2, 0) * cw[0:1] + pltpu.roll(u, 1, 0) * cw[1:2] + u * cw[2:3]
            conv.append(taps[HALO:] + cb_ref[:, col:col + FFN_SUB])
        ug, uv = conv
        act_ref[:, s * FFN_SUB:(s + 1) * FFN_SUB] = (ug * jax.nn.sigmoid(ug) * uv).astype(BF16)
    y = x1_ref[...] + _dot(act_ref[...], wd_ref[...])
    o_ref[...] = _rms(y, gfin_ref[...])


def _ffn(h2, x1, w_up, conv_w, conv_b, w_down, g_final, seq):
    t = h2.shape[0]
    tm = FFN_TM
    hb = tm // HALO
    resident = lambda shape: pl.BlockSpec(shape, lambda i: (0, 0), pipeline_mode=pl.Buffered(1))
    return pl.pallas_call(
        functools.partial(_ffn_kernel, seq // tm),
        grid=(t // tm,),
        in_specs=[
            pl.BlockSpec((tm, D_MODEL), lambda i: (i, 0)),
            pl.BlockSpec((HALO, D_MODEL), lambda i: (jnp.maximum(i * hb - 1, 0), 0)),
            pl.BlockSpec((tm, D_MODEL), lambda i: (i, 0)),
            resident((D_MODEL, 2 * D_FF)),
            resident((TOPK, 2 * D_FF)),
            resident((1, 2 * D_FF)),
            resident((D_FF, D_MODEL)),
            resident((1, D_MODEL)),
        ],
        out_specs=pl.BlockSpec((tm, D_MODEL), lambda i: (i, 0)),
        out_shape=jax.ShapeDtypeStruct((t, D_MODEL), F32),
        scratch_shapes=[pltpu.VMEM((tm, D_FF), BF16)],
        compiler_params=pltpu.CompilerParams(dimension_semantics=("arbitrary",),
                                             vmem_limit_bytes=VMEM_LIMIT),
        name="ffn",
    )(h2, h2, x1, w_up, conv_w, conv_b, w_down, g_final)


def _rope_tables(seq):
    inv = ROPE_THETA ** (-np.arange(HALF, dtype=np.float64) / HALF)
    ang = np.arange(seq, dtype=np.float64)[:, None] * inv[None, :]
    cos, sin = np.cos(ang), np.sin(ang)
    ck = np.concatenate([cos, cos, cos, cos], axis=1)
    sk = np.concatenate([-sin, sin, -sin, sin], axis=1)
    q_scale = SCALE * LOG2E
    return tuple(jnp.asarray(t, F32) for t in (cos.T * q_scale, sin.T * q_scale, ck, sk))


def kernel(x, g_mix, w_in, b_gate, w_branch_a, w_branch_b, w_out, g_ffn, w_up, conv_w, conv_b,
           w_down, g_final):
    batch, seq, _ = x.shape
    assert seq % BLK == 0 and seq % FFN_TM == 0 and seq // BLK <= HALF
    assert g_mix.shape[0] == 1
    x2 = x.reshape(batch * seq, D_MODEL)
    cq, sq, ck, sk = _rope_tables(seq)
    for layer in range(1):
        w = w_in[layer]
        col = lambda k: w[:, k * WIDTH:(k + 1) * WIDTH]
        w_row = jnp.concatenate([col(1), col(4)], axis=1).astype(BF16)
        w_t = jnp.concatenate([col(0), col(2), col(3), col(5)], axis=1).T.astype(BF16)
        w_gate = w[:, 6 * WIDTH:].astype(BF16)
        gm = g_mix[layer][None, :]
        qa, ka, va, qb, kb, vb = _proj(x2, gm, w_row, w_t, cq, sq, ck, sk, batch, seq)
        ya_t = _moba(qa, ka, va, batch, seq)
        yb_t = _sb(qb, kb, vb, batch, seq)
        x1, h2, w_up16, w_down16 = _merge(
            x2, ya_t, yb_t, gm, w_gate, b_gate[layer][None, :],
            w_branch_a[layer].astype(BF16), w_branch_b[layer].astype(BF16),
            w_out[layer].astype(BF16), g_ffn[layer][None, :], w_up[layer], w_down[layer])
        x2 = _ffn(h2, x1, w_up16, conv_w[layer], conv_b[layer][None, :], w_down16,
                  g_final[None, :], seq)
    return x2.reshape(batch, seq, D_MODEL)
```
